```python
import jax, jax.numpy as jnp
from jax import lax
import numpy as np

D_MODEL = 1024
BATCH = 4
SEQ = 4096
DEPTH = 4

N_META = 16
POOL_WIDTH = D_MODEL
POOL_GROUPS = 4
POOL_GROUP_DIM = POOL_WIDTH // POOL_GROUPS
POOL_WINDOWS = (2, 4, 8, 16)
SSM_EXPAND = 2
D_INNER = SSM_EXPAND * D_MODEL
SSM_HEAD_DIM = 64
SSM_HEADS = D_INNER // SSM_HEAD_DIM
SSM_GROUPS = 8
HEADS_PER_GROUP = SSM_HEADS // SSM_GROUPS
D_STATE = 128
CONV_WIDTH = 4
CHUNK = 128
D_XBC = D_INNER + 2 * SSM_GROUPS * D_STATE
D_FF = 4 * D_MODEL
N_BRANCHES = 2
EPS = 1e-5

OFF_POOL = 0
OFF_Z = OFF_POOL + POOL_WIDTH
OFF_XBC = OFF_Z + D_INNER
OFF_DT = OFF_XBC + D_XBC
OFF_GATE = OFF_DT + SSM_HEADS
IN_COLS = OFF_GATE + N_BRANCHES * D_MODEL

kernel_name = "hybrid_pool_ssd_gated_parallel"


def rmsnorm(x, w):
    xf = x.astype(jnp.float32)
    xf = xf * lax.rsqrt(jnp.mean(xf * xf, axis=-1, keepdims=True) + EPS)
    return xf.astype(x.dtype) * w


def pool_mixer(u, w_group, scale):
    bsz, seqlen, _ = u.shape
    ug = u.reshape(bsz, seqlen, POOL_GROUPS, POOL_GROUP_DIM)
    pos = jnp.arange(seqlen)[None, :, None]
    outs = []
    for g, win in enumerate(POOL_WINDOWS):
        xg = ug[:, :, g, :]
        cs = jnp.cumsum(xg.astype(jnp.float32), axis=1)
        shifted = jnp.pad(cs, ((0, 0), (win, 0), (0, 0)))[:, :seqlen]
        count = jnp.minimum(pos + 1, win).astype(jnp.float32)
        mean = (cs - shifted) / count
        outs.append(mean.astype(u.dtype) - xg)
    pooled = jnp.stack(outs, axis=2)
    y = jnp.einsum("blgc,gcd->blgd", pooled, w_group).reshape(bsz, seqlen, POOL_WIDTH)
    return y * scale


def causal_depthwise_conv(x, w, b):
    seqlen = x.shape[1]
    xp = jnp.pad(x, ((0, 0), (CONV_WIDTH - 1, 0), (0, 0)))
    y = b
    for k in range(CONV_WIDTH):
        y = y + xp[:, k:k + seqlen] * w[k]
    return y


def ssd_chunked(x, dt, a, b_mat, c_mat):
    bsz, seqlen = x.shape[0], x.shape[1]
    pad = (-N_META) % CHUNK
    lp = seqlen + pad
    nc = lp // CHUNK

    def fpad(t):
        return jnp.pad(t, [(0, 0), (pad, 0)] + [(0, 0)] * (t.ndim - 2))

    xf = fpad(x).astype(jnp.float32)
    dtf = fpad(dt).astype(jnp.float32)
    bf = fpad(b_mat).astype(jnp.float32).reshape(bsz, nc, CHUNK, SSM_GROUPS, D_STATE)
    cf = fpad(c_mat).astype(jnp.float32).reshape(bsz, nc, CHUNK, SSM_GROUPS, D_STATE)
    xdt = (xf * dtf[..., None]).reshape(bsz, nc, CHUNK, SSM_GROUPS, HEADS_PER_GROUP, SSM_HEAD_DIM)
    a_dt = (dtf * a.astype(jnp.float32)).reshape(bsz, nc, CHUNK, SSM_GROUPS, HEADS_PER_GROUP)
    a_cs = jnp.cumsum(a_dt, axis=2)

    mask = jnp.tril(jnp.ones((CHUNK, CHUNK), dtype=bool))[:, :, None, None]
    diff = a_cs[:, :, :, None] - a_cs[:, :, None, :]
    lmat = jnp.exp(jnp.where(mask, diff, -jnp.inf))
    cb = jnp.einsum("bclgn,bcsgn->bclsg", cf, bf)
    y_diag = jnp.einsum("bclsg,bclsgr,bcsgrp->bclgrp", cb, lmat, xdt)

    decay_states = jnp.exp(a_cs[:, :, -1:] - a_cs)
    states = jnp.einsum("bclgn,bclgr,bclgrp->bcgrpn", bf, decay_states, xdt)
    chunk_decay = jnp.exp(a_cs[:, :, -1])

    def step(h, inp):
        dec, st = inp
        return dec[..., None, None] * h + st, h

    h0 = jnp.zeros((bsz, SSM_GROUPS, HEADS_PER_GROUP, SSM_HEAD_DIM, D_STATE), jnp.float32)
    _, prev = lax.scan(step, h0, (jnp.moveaxis(chunk_decay, 1, 0), jnp.moveaxis(states, 1, 0)))
    prev = jnp.moveaxis(prev, 0, 1)

    y_off = jnp.einsum("bclgn,bcgrpn,bclgr->bclgrp", cf, prev, jnp.exp(a_cs))
    y = (y_diag + y_off).reshape(bsz, lp, SSM_HEADS, SSM_HEAD_DIM)[:, pad:]
    return y.astype(x.dtype)


def mamba2_branch(z, xbc, dt_raw, conv_w, conv_b, dt_bias, a_log, d_skip, norm_w):
    bsz, seqlen, _ = z.shape
    xbc = jax.nn.silu(causal_depthwise_conv(xbc, conv_w, conv_b))
    xs = xbc[..., :D_INNER].reshape(bsz, seqlen, SSM_HEADS, SSM_HEAD_DIM)
    b_mat = xbc[..., D_INNER:D_INNER + SSM_GROUPS * D_STATE].reshape(bsz, seqlen, SSM_GROUPS, D_STATE)
    c_mat = xbc[..., D_INNER + SSM_GROUPS * D_STATE:].reshape(bsz, seqlen, SSM_GROUPS, D_STATE)
    dt = jax.nn.softplus(dt_raw + dt_bias)
    a = -jnp.exp(a_log)
    y = ssd_chunked(xs, dt, a, b_mat, c_mat) + xs * d_skip[:, None]
    y = y.reshape(bsz, seqlen, D_INNER) * jax.nn.silu(z)
    yg = y.reshape(bsz, seqlen, SSM_GROUPS, D_INNER // SSM_GROUPS).astype(jnp.float32)
    yg = yg * lax.rsqrt(jnp.mean(yg * yg, axis=-1, keepdims=True) + EPS)
    return yg.reshape(bsz, seqlen, D_INNER).astype(z.dtype) * norm_w


def setup_inputs(seed: int = 0) -> dict:
    key = jax.random.key(seed)
    ks = jax.random.split(key, 24)
    f32 = jnp.float32
    nrm = lambda k, shape, s: jax.random.normal(k, shape, f32) * s
    dt_init = jnp.exp(jax.random.uniform(ks[9], (DEPTH, SSM_HEADS), f32, np.log(1e-3), np.log(1e-1)))
    return {
        "x": nrm(ks[0], (BATCH, SEQ, D_MODEL), 1.0),
        "meta_tokens": nrm(ks[1], (N_META, D_MODEL), 1.0),
        "mix_norm_w": 1.0 + nrm(ks[2], (DEPTH, D_MODEL), 0.02),
        "w_in": nrm(ks[3], (DEPTH, D_MODEL, IN_COLS), D_MODEL ** -0.5),
        "b_gate": nrm(ks[4], (DEPTH, N_BRANCHES * D_MODEL), 0.02),
        "pool_w_group": nrm(ks[5], (DEPTH, POOL_GROUPS, POOL_GROUP_DIM, POOL_GROUP_DIM), POOL_GROUP_DIM ** -0.5),
        "pool_scale": 1.0 + nrm(ks[6], (DEPTH, POOL_WIDTH), 0.02),
        "w_pool_up": nrm(ks[7], (DEPTH, POOL_WIDTH, D_MODEL), POOL_WIDTH ** -0.5),
        "conv_w": nrm(ks[8], (DEPTH, CONV_WIDTH, D_XBC), CONV_WIDTH ** -0.5),
        "conv_b": nrm(ks[10], (DEPTH, D_XBC), 0.02),
        "dt_bias": dt_init + jnp.log(-jnp.expm1(-dt_init)),
        "a_log": jnp.log(jax.random.uniform(ks[11], (DEPTH, SSM_HEADS), f32, 1.0, 16.0)),
        "d_skip": 1.0 + nrm(ks[12], (DEPTH, SSM_HEADS), 0.02),
        "ssd_norm_w": 1.0 + nrm(ks[13], (DEPTH, D_INNER), 0.02),
        "w_ssd_out": nrm(ks[14], (DEPTH, D_INNER, D_MODEL), D_INNER ** -0.5),
        "w_o": nrm(ks[15], (DEPTH, D_MODEL, D_MODEL), D_MODEL ** -0.5),
        "mlp_norm_w": 1.0 + nrm(ks[16], (DEPTH, D_MODEL), 0.02),
        "w_ff1": nrm(ks[17], (DEPTH, D_MODEL, D_FF), D_MODEL ** -0.5),
        "w_ff2": nrm(ks[18], (DEPTH, D_FF, D_MODEL), 0.5 * D_FF ** -0.5),
        "final_norm_w": 1.0 + nrm(ks[19], (D_MODEL,), 0.02),
    }


def reference(x, meta_tokens, mix_norm_w, w_in, b_gate, pool_w_group, pool_scale, w_pool_up,
              conv_w, conv_b, dt_bias, a_log, d_skip, ssd_norm_w, w_ssd_out, w_o,
              mlp_norm_w, w_ff1, w_ff2, final_norm_w):
    bsz = x.shape[0]
    meta = jnp.broadcast_to(meta_tokens[None].astype(x.dtype), (bsz, N_META, D_MODEL))
    h = jnp.concatenate([meta, x], axis=1)
    for i in range(DEPTH):
        u = rmsnorm(h, mix_norm_w[i])
        proj = u @ w_in[i]
        u_pool = proj[..., OFF_POOL:OFF_Z]
        z = proj[..., OFF_Z:OFF_XBC]
        xbc = proj[..., OFF_XBC:OFF_DT]
        dt_raw = proj[..., OFF_DT:OFF_GATE]
        gates = jax.nn.sigmoid(proj[..., OFF_GATE:] + b_gate[i])
        gate_pool = gates[..., :D_MODEL]
        gate_ssd = gates[..., D_MODEL:]

        y_pool = pool_mixer(u_pool, pool_w_group[i], pool_scale[i]) @ w_pool_up[i]
        y_ssd = mamba2_branch(z, xbc, dt_raw, conv_w[i], conv_b[i], dt_bias[i], a_log[i],
                              d_skip[i], ssd_norm_w[i]) @ w_ssd_out[i]
        h = h + (gate_pool * y_pool + gate_ssd * y_ssd) @ w_o[i]

        v = rmsnorm(h, mlp_norm_w[i])
        hid = jax.nn.relu(v @ w_ff1[i])
        h = h + (hid * hid) @ w_ff2[i]
    out = rmsnorm(h, final_norm_w)
    return out[:, N_META:]
```

```python
import functools

import jax
import jax.numpy as jnp
from jax.experimental import pallas as pl
from jax.experimental.pallas import tpu as pltpu

D_MODEL = 1024
DEPTH = 4
N_META = 16
POOL_GROUPS = 4
POOL_GROUP_DIM = 256
POOL_WINDOWS = (2, 4, 8, 16)
D_INNER = 2048
SSM_HEAD_DIM = 64
SSM_HEADS = 32
SSM_GROUPS = 8
HEADS_PER_GROUP = 4
D_STATE = 128
CONV_WIDTH = 4
CHUNK = 128
D_FF = 4096
EPS = 1e-5

OFF_POOL = 0
OFF_Z = 1024
OFF_XBC = 3072
OFF_DT = 7168
OFF_GATE = 7200

COL_XS = 0
COL_Z = 2048
COL_GATE = 4096
COL_B = 6144
COL_C = 7168
COL_POOL = 8192
N_MAIN = 9216
DT_LANES = 128

PAD_FRONT = CHUNK - N_META
HALO = 16
GROUP_X = HEADS_PER_GROUP * SSM_HEAD_DIM

TM_PROJ = 1056
TN_PROJ = 2304
TM_POOL = 384
TM_MLP = 384
FF_CHUNK = 1024
VMEM_LIMIT = 56 * 1024 * 1024

f32 = jnp.float32
bf16 = jnp.bfloat16


def _dot(a, b):
    return jnp.dot(a, b, preferred_element_type=f32)


def _split3(x):
    p1 = x.astype(bf16)
    r1 = x - p1.astype(f32)
    p2 = r1.astype(bf16)
    p3 = (r1 - p2.astype(f32)).astype(bf16)
    return p1, p2, p3


def _dot_exact_rhs(x, e):
    p1, p2, p3 = _split3(x)
    return _dot(p1, e) + _dot(p2, e) + _dot(p3, e)


def _dot_exact_lhs(e, x):
    p1, p2, p3 = _split3(x)
    return _dot(e, p1) + _dot(e, p2) + _dot(e, p3)


def _sigmoid(x):
    return 1.0 / (1.0 + jnp.exp(-x))


def _silu(x):
    return x * _sigmoid(x)


def _softplus(x):
    return jnp.maximum(x, 0.0) + jnp.log1p(jnp.exp(-jnp.abs(x)))


def _in_proj_kernel(h_ref, nw_ref, w_ref, wdt_ref, o_ref, dt_ref, u_scr):
    i = pl.program_id(1)
    j = pl.program_id(2)

    @pl.when(j == 0)
    def _():
        x = h_ref[0]
        ms = jnp.mean(x * x, axis=-1, keepdims=True)
        u = x * jax.lax.rsqrt(ms + EPS) * nw_ref[...]
        pos = i * TM_PROJ + jax.lax.broadcasted_iota(jnp.int32, (TM_PROJ, 1), 0)
        u = jnp.where(pos >= PAD_FRONT, u, 0.0).astype(bf16)
        u_scr[...] = u
        dt_ref[0] = _dot(u, wdt_ref[...])

    o_ref[0] = _dot(u_scr[...], w_ref[...]).astype(bf16)


def _in_proj(h, nw, w_main, w_dt):
    bsz, lp, _ = h.shape
    grid = (bsz, lp // TM_PROJ, N_MAIN // TN_PROJ)
    return pl.pallas_call(
        _in_proj_kernel,
        grid=grid,
        in_specs=[
            pl.BlockSpec((1, TM_PROJ, D_MODEL), lambda b, i, j: (b, i, 0)),
            pl.BlockSpec((1, D_MODEL), lambda b, i, j: (0, 0)),
            pl.BlockSpec((D_MODEL, TN_PROJ), lambda b, i, j: (0, j)),
            pl.BlockSpec((D_MODEL, DT_LANES), lambda b, i, j: (0, 0)),
        ],
        out_specs=[
            pl.BlockSpec((1, TM_PROJ, TN_PROJ), lambda b, i, j: (b, i, j)),
            pl.BlockSpec((1, TM_PROJ, DT_LANES), lambda b, i, j: (b, i, 0)),
        ],
        out_shape=[
            jax.ShapeDtypeStruct((bsz, lp, N_MAIN), bf16),
            jax.ShapeDtypeStruct((bsz, lp, DT_LANES), f32),
        ],
        scratch_shapes=[pltpu.VMEM((TM_PROJ, D_MODEL), bf16)],
        compiler_params=pltpu.CompilerParams(
            dimension_semantics=("parallel", "parallel", "arbitrary"),
            vmem_limit_bytes=VMEM_LIMIT),
        name="in_proj",
    )(h, nw, w_main, w_dt)


def _pool_kernel(u_ref, halo_ref, wg_ref, scale_ref, wup_ref, o_ref, ext_scr, pooled_scr, y1_scr):
    i = pl.program_id(1)
    halo = halo_ref[0]
    ext_scr[0:HALO, :] = jnp.where(i > 0, halo, jnp.zeros_like(halo))
    ext_scr[HALO:, :] = u_ref[0]

    t_io = jax.lax.broadcasted_iota(jnp.int32, (CHUNK, CHUNK + HALO), 0)
    s_io = jax.lax.broadcasted_iota(jnp.int32, (CHUNK, CHUNK + HALO), 1)
    row_io = jax.lax.broadcasted_iota(jnp.int32, (CHUNK, 1), 0)
    for g, win in enumerate(POOL_WINDOWS):
        band = jnp.where((s_io <= t_io + HALO) & (s_io > t_io + HALO - win), 1.0, 0.0).astype(bf16)
        cols = slice(g * POOL_GROUP_DIM, (g + 1) * POOL_GROUP_DIM)
        for sb in range(TM_POOL // CHUNK):
            e = ext_scr[sb * CHUNK:sb * CHUNK + CHUNK + HALO, cols]
            wsum = _dot(band, e)
            tok = i * TM_POOL + sb * CHUNK + row_io - PAD_FRONT
            cnt = jnp.clip(tok + 1, 1, win).astype(f32)
            x = e[HALO:, :].astype(f32)
            pooled_scr[sb * CHUNK:(sb + 1) * CHUNK, cols] = (wsum / cnt - x).astype(bf16)
        y1 = _dot(pooled_scr[:, cols], wg_ref[g]) * scale_ref[:, cols]
        y1_scr[:, cols] = y1.astype(bf16)
    o_ref[0] = _dot(y1_scr[...], wup_ref[...]).astype(bf16)


def _pool(proj, wg, scale, wup):
    bsz, lp, _ = proj.shape
    rb = TM_POOL // HALO
    return pl.pallas_call(
        _pool_kernel,
        grid=(bsz, lp // TM_POOL),
        in_specs=[
            pl.BlockSpec((1, TM_POOL, D_MODEL), lambda b, i: (b, i, COL_POOL // D_MODEL)),
            pl.BlockSpec((1, HALO, D_MODEL),
                         lambda b, i: (b, jnp.maximum(i * rb - 1, 0), COL_POOL // D_MODEL)),
            pl.BlockSpec((POOL_GROUPS, POOL_GROUP_DIM, POOL_GROUP_DIM), lambda b, i: (0, 0, 0)),
            pl.BlockSpec((1, D_MODEL), lambda b, i: (0, 0)),
            pl.BlockSpec((D_MODEL, D_MODEL), lambda b, i: (0, 0)),
        ],
        out_specs=pl.BlockSpec((1, TM_POOL, D_MODEL), lambda b, i: (b, i, 0)),
        out_shape=jax.ShapeDtypeStruct((bsz, lp, D_MODEL), bf16),
        scratch_shapes=[
            pltpu.VMEM((TM_POOL + HALO, D_MODEL), bf16),
            pltpu.VMEM((TM_POOL, D_MODEL), bf16),
            pltpu.VMEM((TM_POOL, D_MODEL), bf16),
        ],
        compiler_params=pltpu.CompilerParams(
            dimension_semantics=("parallel", "parallel"),
            vmem_limit_bytes=VMEM_LIMIT),
        name="pool",
    )(proj, proj, wg, scale, wup)


def _conv_silu(ext_scr, halo, cur, w, b, use_halo):
    n = cur.shape[1]
    ext_scr[0:HALO, 0:n] = jnp.where(use_halo, halo.astype(f32), 0.0)
    ext_scr[HALO:, 0:n] = cur.astype(f32)
    acc = b
    for k in range(CONV_WIDTH):
        off = HALO - (CONV_WIDTH - 1) + k
        acc = acc + ext_scr[off:off + CHUNK, 0:n] * w[k:k + 1, :]
    return _silu(acc)


def _ssd_kernel(xs_ref, xs_h_ref, b_ref, b_h_ref, c_ref, c_h_ref, z_ref, dt_ref,
                cw_ref, cb_ref, dtb_ref, alog_ref, dskip_ref, nw_ref, eexp_ref,
                o_ref, state_scr, ext_scr):
    c = pl.program_id(1)

    @pl.when(c == 0)
    def _():
        state_scr[...] = jnp.zeros_like(state_scr)

    use_halo = c > 0
    row = jax.lax.broadcasted_iota(jnp.int32, (CHUNK, 1), 0)
    valid = jnp.logical_or(c > 0, row >= PAD_FRONT)

    lane = jax.lax.broadcasted_iota(jnp.int32, (1, DT_LANES), 1)
    dt = _softplus(dt_ref[0] + dtb_ref[...])
    dt = jnp.where(jnp.logical_and(valid, lane < SSM_HEADS), dt, 0.0)
    a = -jnp.exp(alog_ref[...])
    a_dt = dt * a
    r_io = jax.lax.broadcasted_iota(jnp.int32, (CHUNK, CHUNK), 0)
    c_io = jax.lax.broadcasted_iota(jnp.int32, (CHUNK, CHUNK), 1)
    causal = r_io >= c_io
    tri = jnp.where(causal, 1.0, 0.0).astype(bf16)
    tri_t = jnp.where(r_io <= c_io, 1.0, 0.0).astype(bf16)
    acs = _dot_exact_lhs(tri, a_dt)
    acs_t = _dot_exact_rhs(a_dt.T, tri_t)
    dt_t = dt.T
    acs_last = acs[CHUNK - 1:CHUNK, :]
    w_state = jnp.exp(acs_last - acs) * dt
    e_acs = jnp.exp(acs)
    chunk_decay = jnp.exp(jnp.broadcast_to(acs_last, (8, DT_LANES)))
    cd_exp = _dot_exact_rhs(chunk_decay, eexp_ref[...])[0:1, :]

    lane_x = jax.lax.broadcasted_iota(jnp.int32, (1, GROUP_X), 1) // SSM_HEAD_DIM

    def expand(cols4):
        out = cols4[:, 3:4]
        for r in (2, 1, 0):
            out = jnp.where(lane_x == r, cols4[:, r:r + 1], out)
        return out

    for g in range(SSM_GROUPS):
        xcols = slice(g * GROUP_X, (g + 1) * GROUP_X)
        ncols = slice(g * D_STATE, (g + 1) * D_STATE)
        bc_b = slice(D_INNER + g * D_STATE, D_INNER + (g + 1) * D_STATE)
        bc_c = slice(D_INNER + SSM_GROUPS * D_STATE + g * D_STATE,
                     D_INNER + SSM_GROUPS * D_STATE + (g + 1) * D_STATE)
        xg = _conv_silu(ext_scr, xs_h_ref[0, :, xcols], xs_ref[0, :, xcols],
                        cw_ref[:, xcols], cb_ref[:, xcols], use_halo)
        bg = _conv_silu(ext_scr, b_h_ref[0, :, ncols], b_ref[0, :, ncols],
                        cw_ref[:, bc_b], cb_ref[:, bc_b], use_halo)
        cg = _conv_silu(ext_scr, c_h_ref[0, :, ncols], c_ref[0, :, ncols],
                        cw_ref[:, bc_c], cb_ref[:, bc_c], use_halo)
        xg = jnp.where(valid, xg, 0.0)
        bg = jnp.where(valid, bg, 0.0).astype(bf16)
        cg = jnp.where(valid, cg, 0.0).astype(bf16)
        xg_b = xg.astype(bf16)

        cb = jax.lax.dot_general(cg, bg, (((1,), (1,)), ((), ())), preferred_element_type=f32)
        heads = slice(g * HEADS_PER_GROUP, (g + 1) * HEADS_PER_GROUP)
        y = None
        for r in range(HEADS_PER_GROUP):
            h = g * HEADS_PER_GROUP + r
            diff = acs[:, h:h + 1] - acs_t[h:h + 1, :]
            m = jnp.where(causal, jnp.exp(diff), 0.0) * cb * dt_t[h:h + 1, :]
            xr = jnp.where(lane_x == r, xg_b, jnp.zeros_like(xg_b))
            yd = _dot(m.astype(bf16), xr)
            y = yd if y is None else y + yd

        s_old = state_scr[g]
        y = y + _dot(cg, s_old.astype(bf16)) * expand(e_acs[:, heads])
        xw = (xg * expand(w_state[:, heads])).astype(bf16)
        st = jax.lax.dot_general(bg, xw, (((0,), (0,)), ((), ())), preferred_element_type=f32)
        state_scr[g] = s_old * cd_exp[:, xcols] + st

        y = y + xg * dskip_ref[:, xcols]
        y = y * _silu(z_ref[0, :, xcols].astype(f32))
        ms = jnp.mean(y * y, axis=-1, keepdims=True)
        y = y * jax.lax.rsqrt(ms + EPS) * nw_ref[:, xcols]
        o_ref[0, :, xcols] = y.astype(bf16)


def _ssd(proj, dt_raw, cw, cb, dtb, alog, dskip, nw, eexp):
    bsz, lp, _ = proj.shape
    rb = CHUNK // HALO

    def cur(width, col):
        return pl.BlockSpec((1, CHUNK, width), lambda b, c: (b, c, col // width))

    def halo(width, col):
        return pl.BlockSpec((1, HALO, width),
                            lambda b, c: (b, jnp.maximum(c * rb - 1, 0), col // width))

    def const(shape):
        return pl.BlockSpec(shape, lambda b, c: (0,) * len(shape))

    n_bc = SSM_GROUPS * D_STATE
    return pl.pallas_call(
        _ssd_kernel,
        grid=(bsz, lp // CHUNK),
        in_specs=[
            cur(D_INNER, COL_XS), halo(D_INNER, COL_XS),
            cur(n_bc, COL_B), halo(n_bc, COL_B),
            cur(n_bc, COL_C), halo(n_bc, COL_C),
            cur(D_INNER, COL_Z),
            pl.BlockSpec((1, CHUNK, DT_LANES), lambda b, c: (b, c, 0)),
            const((CONV_WIDTH, D_INNER + 2 * n_bc)),
            const((1, D_INNER + 2 * n_bc)),
            const((1, DT_LANES)),
            const((1, DT_LANES)),
            const((1, D_INNER)),
            const((1, D_INNER)),
            const((DT_LANES, D_INNER)),
        ],
        out_specs=pl.BlockSpec((1, CHUNK, D_INNER), lambda b, c: (b, c, 0)),
        out_shape=jax.ShapeDtypeStruct((bsz, lp, D_INNER), bf16),
        scratch_shapes=[
            pltpu.VMEM((SSM_GROUPS, D_STATE, GROUP_X), f32),
            pltpu.VMEM((CHUNK + HALO, GROUP_X), f32),
        ],
        compiler_params=pltpu.CompilerParams(
            dimension_semantics=("parallel", "arbitrary"),
            vmem_limit_bytes=VMEM_LIMIT),
        name="ssd",
    )(proj, proj, proj, proj, proj, proj, proj, dt_raw, cw, cb, dtb, alog, dskip, nw, eexp)


def _merge_mlp_kernel(h_ref, gate_ref, ypool_ref, ynorm_ref, bgate_ref, wssd_ref, wo_ref,
                      mw_ref, ff1_ref, ff2_ref, o_ref):
    y_ssd = _dot(ynorm_ref[0], wssd_ref[...])
    gates = _sigmoid(gate_ref[0].astype(f32) + bgate_ref[...])
    mix = gates[:, :D_MODEL] * ypool_ref[0].astype(f32) + gates[:, D_MODEL:] * y_ssd
    h1 = h_ref[0] + _dot(mix.astype(bf16), wo_ref[...])
    ms = jnp.mean(h1 * h1, axis=-1, keepdims=True)
    v = (h1 * jax.lax.rsqrt(ms + EPS) * mw_ref[...]).astype(bf16)
    acc = h1
    for k in range(D_FF // FF_CHUNK):
        hid = jnp.maximum(_dot(v, ff1_ref[:, k * FF_CHUNK:(k + 1) * FF_CHUNK]), 0.0)
        acc = acc + _dot((hid * hid).astype(bf16), ff2_ref[k * FF_CHUNK:(k + 1) * FF_CHUNK, :])
    o_ref[0] = acc


def _merge_mlp(h, proj, ypool, ynorm, bgate, wssd, wo, mw, ff1, ff2):
    bsz, lp, _ = h.shape

    def const(shape):
        return pl.BlockSpec(shape, lambda b, i: (0,) * len(shape), pipeline_mode=pl.Buffered(1))

    return pl.pallas_call(
        _merge_mlp_kernel,
        grid=(bsz, lp // TM_MLP),
        in_specs=[
            pl.BlockSpec((1, TM_MLP, D_MODEL), lambda b, i: (b, i, 0)),
            pl.BlockSpec((1, TM_MLP, 2 * D_MODEL), lambda b, i: (b, i, COL_GATE // (2 * D_MODEL))),
            pl.BlockSpec((1, TM_MLP, D_MODEL), lambda b, i: (b, i, 0)),
            pl.BlockSpec((1, TM_MLP, D_INNER), lambda b, i: (b, i, 0)),
            const((1, 2 * D_MODEL)),
            const((D_INNER, D_MODEL)),
            const((D_MODEL, D_MODEL)),
            const((1, D_MODEL)),
            const((D_MODEL, D_FF)),
            const((D_FF, D_MODEL)),
        ],
        out_specs=pl.BlockSpec((1, TM_MLP, D_MODEL), lambda b, i: (b, i, 0)),
        out_shape=jax.ShapeDtypeStruct((bsz, lp, D_MODEL), f32),
        compiler_params=pltpu.CompilerParams(
            dimension_semantics=("parallel", "parallel"),
            vmem_limit_bytes=VMEM_LIMIT),
        name="merge_mlp",
    )(h, proj, ypool, ynorm, bgate, wssd, wo, mw, ff1, ff2)


def _final_norm_kernel(h_ref, w_ref, o_ref):
    x = h_ref[0]
    ms = jnp.mean(x * x, axis=-1, keepdims=True)
    o_ref[0] = x * jax.lax.rsqrt(ms + EPS) * w_ref[...]


def _final_norm(h, w, seq):
    bsz = h.shape[0]
    return pl.pallas_call(
        _final_norm_kernel,
        grid=(bsz, seq // CHUNK),
        in_specs=[
            pl.BlockSpec((1, CHUNK, D_MODEL), lambda b, i: (b, i + 1, 0)),
            pl.BlockSpec((1, D_MODEL), lambda b, i: (0, 0)),
        ],
        out_specs=pl.BlockSpec((1, CHUNK, D_MODEL), lambda b, i: (b, i, 0)),
        out_shape=jax.ShapeDtypeStruct((bsz, seq, D_MODEL), f32),
        compiler_params=pltpu.CompilerParams(dimension_semantics=("parallel", "parallel")),
        name="final_norm",
    )(h, w)


def kernel(x, meta_tokens, mix_norm_w, w_in, b_gate, pool_w_group, pool_scale, w_pool_up,
           conv_w, conv_b, dt_bias, a_log, d_skip, ssd_norm_w, w_ssd_out, w_o,
           mlp_norm_w, w_ff1, w_ff2, final_norm_w):
    bsz, seq, _ = x.shape
    meta = jnp.broadcast_to(meta_tokens[None].astype(x.dtype), (bsz, N_META, D_MODEL))
    h = jnp.concatenate([jnp.zeros((bsz, PAD_FRONT, D_MODEL), x.dtype), meta, x], axis=1)

    n_bc = SSM_GROUPS * D_STATE
    w_main = jnp.concatenate([
        w_in[:, :, OFF_XBC:OFF_XBC + D_INNER],
        w_in[:, :, OFF_Z:OFF_XBC],
        w_in[:, :, OFF_GATE:],
        w_in[:, :, OFF_XBC + D_INNER:OFF_DT],
        w_in[:, :, OFF_POOL:OFF_Z],
    ], axis=-1).astype(bf16)
    w_dt = jnp.pad(w_in[:, :, OFF_DT:OFF_GATE], ((0, 0), (0, 0), (0, DT_LANES - SSM_HEADS))).astype(bf16)
    pad_h = ((0, 0), (0, DT_LANES - SSM_HEADS))
    dtb = jnp.pad(dt_bias, pad_h)
    alog = jnp.pad(a_log, pad_h)
    dskip = jnp.repeat(d_skip, SSM_HEAD_DIM, axis=-1)
    eexp = (jnp.arange(DT_LANES)[:, None] == (jnp.arange(D_INNER)[None, :] // SSM_HEAD_DIM)).astype(bf16)
    wg = pool_w_group.astype(bf16)
    wup = w_pool_up.astype(bf16)
    wssd = w_ssd_out.astype(bf16)
    wo = w_o.astype(bf16)
    ff1 = w_ff1.astype(bf16)
    ff2 = w_ff2.astype(bf16)

    for i in range(DEPTH):
        proj, dt_raw = _in_proj(h, mix_norm_w[i][None], w_main[i], w_dt[i])
        ypool = _pool(proj, wg[i], pool_scale[i][None], wup[i])
        ynorm = _ssd(proj, dt_raw, conv_w[i], conv_b[i][None], dtb[i][None], alog[i][None],
                     dskip[i][None], ssd_norm_w[i][None], eexp)
        h = _merge_mlp(h, proj, ypool, ynorm, b_gate[i][None], wssd[i], wo[i],
                       mlp_norm_w[i][None], ff1[i], ff2[i])
    return _final_norm(h, final_norm_w[None], seq)
```

```python
import math

import jax
import jax.numpy as jnp
from jax.experimental import pallas as pl
from jax.experimental.pallas import tpu as pltpu

D_MODEL = 1024
DEPTH = 4
N_META = 16
POOL_GROUPS = 4
POOL_GROUP_DIM = 256
POOL_WINDOWS = (2, 4, 8, 16)
D_INNER = 2048
SSM_HEAD_DIM = 64
SSM_HEADS = 32
SSM_GROUPS = 8
HEADS_PER_GROUP = 4
D_STATE = 128
CONV_WIDTH = 4
CHUNK = 128
D_FF = 4096
EPS = 1e-5
LOG2E = math.log2(math.e)

OFF_POOL = 0
OFF_Z = 1024
OFF_XBC = 3072
OFF_DT = 7168
OFF_GATE = 7200

COL_XS = 0
COL_Z = 2048
COL_GATE = 4096
COL_B = 6144
COL_C = 7168
COL_POOL = 8192
N_MAIN = 9216
DT_LANES = 128

PAD_FRONT = CHUNK - N_META
HALO = 16
SUBLANES = 8
LANES = 128
GROUP_X = HEADS_PER_GROUP * SSM_HEAD_DIM

TM_PROJ = 1056
TN_PROJ = 2304
TM_POOL = 384
TM_MLP = 384
TM_FINAL = 512
FF_CHUNK = 1024
VMEM_LIMIT = 56 * 1024 * 1024

f32 = jnp.float32
bf16 = jnp.bfloat16


def _dot(a, b):
    return jnp.dot(a, b, preferred_element_type=f32)


def _split3(x):
    p1 = x.astype(bf16)
    r1 = x - p1.astype(f32)
    p2 = r1.astype(bf16)
    p3 = (r1 - p2.astype(f32)).astype(bf16)
    return p1, p2, p3


def _dot_exact_rhs(x, e):
    p1, p2, p3 = _split3(x)
    return _dot(p1, e) + _dot(p2, e) + _dot(p3, e)


def _sigmoid(x):
    return 1.0 / (1.0 + jnp.exp(-x))


def _silu_of_twice(hx):
    return hx + hx * jnp.tanh(hx)


def _softplus(x):
    return jnp.maximum(x, 0.0) + jnp.log1p(jnp.exp(-jnp.abs(x)))


def _in_proj_kernel(h_ref, nw_ref, w_ref, wdt_ref, o_ref, dt_ref, u_scr):
    i = pl.program_id(1)
    j = pl.program_id(2)

    @pl.when(j == 0)
    def _():
        x = h_ref[0]
        ms = jnp.mean(x * x, axis=-1, keepdims=True)
        u = x * jax.lax.rsqrt(ms + EPS) * nw_ref[...]
        pos = i * TM_PROJ + jax.lax.broadcasted_iota(jnp.int32, (TM_PROJ, 1), 0)
        u = jnp.where(pos >= PAD_FRONT, u, 0.0).astype(bf16)
        u_scr[...] = u
        dt_ref[0] = _dot(u, wdt_ref[...])

    o_ref[0] = _dot(u_scr[...], w_ref[...]).astype(bf16)


def _in_proj(layer, h, nw, w_main, w_dt):
    bsz, lp, _ = h.shape
    grid = (bsz, lp // TM_PROJ, N_MAIN // TN_PROJ)
    return pl.pallas_call(
        _in_proj_kernel,
        grid=grid,
        in_specs=[
            pl.BlockSpec((1, TM_PROJ, D_MODEL), lambda b, i, j: (b, i, 0)),
            pl.BlockSpec((None, 1, D_MODEL), lambda b, i, j: (layer, 0, 0)),
            pl.BlockSpec((None, D_MODEL, TN_PROJ), lambda b, i, j: (layer, 0, j)),
            pl.BlockSpec((None, D_MODEL, DT_LANES), lambda b, i, j: (layer, 0, 0)),
        ],
        out_specs=[
            pl.BlockSpec((1, TM_PROJ, TN_PROJ), lambda b, i, j: (b, i, j)),
            pl.BlockSpec((1, TM_PROJ, DT_LANES), lambda b, i, j: (b, i, 0)),
        ],
        out_shape=[
            jax.ShapeDtypeStruct((bsz, lp, N_MAIN), bf16),
            jax.ShapeDtypeStruct((bsz, lp, DT_LANES), f32),
        ],
        scratch_shapes=[pltpu.VMEM((TM_PROJ, D_MODEL), bf16)],
        compiler_params=pltpu.CompilerParams(
            dimension_semantics=("parallel", "parallel", "arbitrary"),
            vmem_limit_bytes=VMEM_LIMIT),
        name="in_proj",
    )(h, nw, w_main, w_dt)


def _pool_kernel(u_ref, halo_ref, wg_ref, scale_ref, wup_ref, o_ref, ext_scr, pooled_scr, y1_scr):
    i = pl.program_id(1)
    halo = halo_ref[0]
    ext_scr[0:HALO, :] = jnp.where(i > 0, halo, jnp.zeros_like(halo))
    ext_scr[HALO:, :] = u_ref[0]

    t_io = jax.lax.broadcasted_iota(jnp.int32, (CHUNK, CHUNK + HALO), 0)
    s_io = jax.lax.broadcasted_iota(jnp.int32, (CHUNK, CHUNK + HALO), 1)
    row_io = jax.lax.broadcasted_iota(jnp.int32, (CHUNK, 1), 0)
    for g, win in enumerate(POOL_WINDOWS):
        band = jnp.where((s_io <= t_io + HALO) & (s_io > t_io + HALO - win), 1.0, 0.0).astype(bf16)
        cols = slice(g * POOL_GROUP_DIM, (g + 1) * POOL_GROUP_DIM)
        for sb in range(TM_POOL // CHUNK):
            e = ext_scr[sb * CHUNK:sb * CHUNK + CHUNK + HALO, cols]
            wsum = _dot(band, e)
            tok = i * TM_POOL + sb * CHUNK + row_io - PAD_FRONT
            cnt = jnp.clip(tok + 1, 1, win).astype(f32)
            x = e[HALO:, :].astype(f32)
            pooled_scr[sb * CHUNK:(sb + 1) * CHUNK, cols] = (wsum / cnt - x).astype(bf16)
        y1 = _dot(pooled_scr[:, cols], wg_ref[g]) * scale_ref[:, cols]
        y1_scr[:, cols] = y1.astype(bf16)
    o_ref[0] = _dot(y1_scr[...], wup_ref[...]).astype(bf16)


def _pool(layer, proj, wg, scale, wup):
    bsz, lp, _ = proj.shape
    rb = TM_POOL // HALO
    return pl.pallas_call(
        _pool_kernel,
        grid=(bsz, lp // TM_POOL),
        in_specs=[
            pl.BlockSpec((1, TM_POOL, D_MODEL), lambda b, i: (b, i, COL_POOL // D_MODEL)),
            pl.BlockSpec((1, HALO, D_MODEL),
                         lambda b, i: (b, jnp.maximum(i * rb - 1, 0), COL_POOL // D_MODEL)),
            pl.BlockSpec((None, POOL_GROUPS, POOL_GROUP_DIM, POOL_GROUP_DIM),
                         lambda b, i: (layer, 0, 0, 0)),
            pl.BlockSpec((None, 1, D_MODEL), lambda b, i: (layer, 0, 0)),
            pl.BlockSpec((None, D_MODEL, D_MODEL), lambda b, i: (layer, 0, 0)),
        ],
        out_specs=pl.BlockSpec((1, TM_POOL, D_MODEL), lambda b, i: (b, i, 0)),
        out_shape=jax.ShapeDtypeStruct((bsz, lp, D_MODEL), bf16),
        scratch_shapes=[
            pltpu.VMEM((TM_POOL + HALO, D_MODEL), bf16),
            pltpu.VMEM((TM_POOL, D_MODEL), bf16),
            pltpu.VMEM((TM_POOL, D_MODEL), bf16),
        ],
        compiler_params=pltpu.CompilerParams(
            dimension_semantics=("parallel", "parallel"),
            vmem_limit_bytes=VMEM_LIMIT),
        name="pool",
    )(proj, proj, wg, scale, wup)


def _conv_silu(halo, cur, w, b, use_halo):
    n = cur.shape[1]
    nreg = CHUNK // SUBLANES
    x3 = cur.astype(f32).reshape(nreg, SUBLANES, n)
    last = halo[HALO - SUBLANES:, :].astype(f32)
    last = jnp.where(use_halo, last, 0.0).reshape(1, SUBLANES, n)
    prev = jnp.concatenate([last, x3[:nreg - 1]], axis=0)
    acc = b.reshape(1, 1, n) + x3 * w[CONV_WIDTH - 1:CONV_WIDTH, :].reshape(1, 1, n)
    for d in range(1, CONV_WIDTH):
        merged = jnp.concatenate([x3[:, :SUBLANES - d, :], prev[:, SUBLANES - d:, :]], axis=1)
        shifted = pltpu.roll(merged, d, axis=1)
        acc = acc + shifted * w[CONV_WIDTH - 1 - d:CONV_WIDTH - d, :].reshape(1, 1, n)
    return _silu_of_twice(acc).reshape(CHUNK, n)


def _ssd_kernel(xs_ref, xs_h_ref, b_ref, b_h_ref, c_ref, c_h_ref, z_ref, dt_ref,
                cw_ref, cb_ref, dtb_ref, alog_ref, dskip_ref, nw_ref, eexp_ref,
                o_ref, state_scr):
    c = pl.program_id(1)

    @pl.when(c == 0)
    def _():
        state_scr[...] = jnp.zeros_like(state_scr)

    use_halo = c > 0

    tok = jax.lax.broadcasted_iota(jnp.int32, (1, CHUNK), 1)
    valid_t = jnp.logical_or(c > 0, tok >= PAD_FRONT)
    raw_t = (dt_ref[0] + dtb_ref[...]).T[0:SSM_HEADS, :]
    dt_t = jnp.where(valid_t, _softplus(raw_t), 0.0)
    a_col = -jnp.exp(alog_ref[...])
    r_io = jax.lax.broadcasted_iota(jnp.int32, (CHUNK, CHUNK), 0)
    c_io = jax.lax.broadcasted_iota(jnp.int32, (CHUNK, CHUNK), 1)
    causal = r_io >= c_io
    tri_t = jnp.where(r_io <= c_io, 1.0, 0.0).astype(bf16)
    acs_t = _dot_exact_rhs(dt_t * a_col, tri_t)
    acs_last = acs_t[:, CHUNK - 1:CHUNK]
    rowarg_t = acs_t * LOG2E - jnp.log(dt_t) * LOG2E
    w_state_t = jnp.exp(acs_last - acs_t) * dt_t
    e_acs_t = jnp.exp(acs_t)

    def tokens_on_rows(x):
        pad = jnp.zeros((CHUNK - SSM_HEADS, CHUNK), f32)
        return jnp.concatenate([x, pad], axis=0).T

    acs2 = tokens_on_rows(acs_t * LOG2E)
    w_state = tokens_on_rows(w_state_t)
    e_acs = tokens_on_rows(e_acs_t)
    chunk_decay = jnp.broadcast_to(e_acs[CHUNK - 1:CHUNK, :], (SUBLANES, DT_LANES))
    cd_exp = _dot_exact_rhs(chunk_decay, eexp_ref[...])[0:1, :]

    low_head = jax.lax.broadcasted_iota(jnp.int32, (1, LANES), 1) < SSM_HEAD_DIM

    for g in range(SSM_GROUPS):
        xcols = slice(g * GROUP_X, (g + 1) * GROUP_X)
        ncols = slice(g * D_STATE, (g + 1) * D_STATE)
        bc_b = slice(D_INNER + g * D_STATE, D_INNER + (g + 1) * D_STATE)
        bc_c = slice(D_INNER + SSM_GROUPS * D_STATE + g * D_STATE,
                     D_INNER + SSM_GROUPS * D_STATE + (g + 1) * D_STATE)
        xg = _conv_silu(xs_h_ref[0, :, xcols], xs_ref[0, :, xcols],
                        cw_ref[:, xcols], cb_ref[:, xcols], use_halo)
        bg = _conv_silu(b_h_ref[0, :, ncols], b_ref[0, :, ncols],
                        cw_ref[:, bc_b], cb_ref[:, bc_b], use_halo).astype(bf16)
        cg = _conv_silu(c_h_ref[0, :, ncols], c_ref[0, :, ncols],
                        cw_ref[:, bc_c], cb_ref[:, bc_c], use_halo)
        cg_b = cg.astype(bf16)
        xg_b = xg.astype(bf16)
        s_old = state_scr[g]
        s_b = s_old.astype(bf16)

        cb = jax.lax.dot_general(cg_b, bg, (((1,), (1,)), ((), ())), preferred_element_type=f32)

        halves = []
        for half in range(HEADS_PER_GROUP // 2):
            hcols = slice(half * LANES, (half + 1) * LANES)
            rhs = jnp.concatenate([xg_b[:, hcols], s_b[:, hcols]], axis=0)
            outs = []
            for r in (2 * half, 2 * half + 1):
                h = g * HEADS_PER_GROUP + r
                lm = jnp.exp2(acs2[:, h:h + 1] - rowarg_t[h:h + 1, :])
                m = (jnp.where(causal, lm, 0.0) * cb).astype(bf16)
                ce = (cg * e_acs[:, h:h + 1]).astype(bf16)
                outs.append(_dot(jnp.concatenate([m, ce], axis=1), rhs))
            halves.append(jnp.where(low_head, outs[0], outs[1]))
        y = jnp.concatenate(halves, axis=1)

        hd = g * HEADS_PER_GROUP
        w_exp = jnp.concatenate([
            jnp.where(low_head, w_state[:, hd + 2 * half:hd + 2 * half + 1],
                      w_state[:, hd + 2 * half + 1:hd + 2 * half + 2])
            for half in range(HEADS_PER_GROUP // 2)], axis=1)
        xw = (xg * w_exp).astype(bf16)
        st = jax.lax.dot_general(bg, xw, (((0,), (0,)), ((), ())), preferred_element_type=f32)
        state_scr[g] = s_old * cd_exp[:, xcols] + st

        y = y + xg * dskip_ref[:, xcols]
        y = y * _silu_of_twice(z_ref[0, :, xcols].astype(f32))
        ms = jnp.mean(y * y, axis=-1, keepdims=True)
        y = y * jax.lax.rsqrt(ms + EPS) * nw_ref[:, xcols]
        o_ref[0, :, xcols] = y.astype(bf16)


def _ssd(layer, proj, dt_raw, cw, cb, dtb, alog, dskip, nw, eexp):
    bsz, lp, _ = proj.shape
    rb = CHUNK // HALO

    def cur(width, col):
        return pl.BlockSpec((1, CHUNK, width), lambda b, c: (b, c, col // width))

    def halo(width, col):
        return pl.BlockSpec((1, HALO, width),
                            lambda b, c: (b, jnp.maximum(c * rb - 1, 0), col // width))

    def const(shape):
        return pl.BlockSpec((None,) + shape, lambda b, c: (layer,) + (0,) * len(shape))

    n_bc = SSM_GROUPS * D_STATE
    return pl.pallas_call(
        _ssd_kernel,
        grid=(bsz, lp // CHUNK),
        in_specs=[
            cur(D_INNER, COL_XS), halo(D_INNER, COL_XS),
            cur(n_bc, COL_B), halo(n_bc, COL_B),
            cur(n_bc, COL_C), halo(n_bc, COL_C),
            cur(D_INNER, COL_Z),
            pl.BlockSpec((1, CHUNK, DT_LANES), lambda b, c: (b, c, 0)),
            const((CONV_WIDTH, D_INNER + 2 * n_bc)),
            const((1, D_INNER + 2 * n_bc)),
            const((1, DT_LANES)),
            const((SSM_HEADS, 1)),
            const((1, D_INNER)),
            const((1, D_INNER)),
            pl.BlockSpec((DT_LANES, D_INNER), lambda b, c: (0, 0)),
        ],
        out_specs=pl.BlockSpec((1, CHUNK, D_INNER), lambda b, c: (b, c, 0)),
        out_shape=jax.ShapeDtypeStruct((bsz, lp, D_INNER), bf16),
        scratch_shapes=[
            pltpu.VMEM((SSM_GROUPS, D_STATE, GROUP_X), f32),
        ],
        compiler_params=pltpu.CompilerParams(
            dimension_semantics=("parallel", "arbitrary"),
            vmem_limit_bytes=VMEM_LIMIT),
        name="ssd",
    )(proj, proj, proj, proj, proj, proj, proj, dt_raw, cw, cb, dtb, alog, dskip, nw, eexp)


def _merge_mlp_kernel(h_ref, gate_ref, ypool_ref, ynorm_ref, bgate_ref, wssd_ref, wo_ref,
                      mw_ref, ff1_ref, ff2_ref, o_ref):
    y_ssd = _dot(ynorm_ref[0], wssd_ref[...])
    gates = _sigmoid(gate_ref[0].astype(f32) + bgate_ref[...])
    mix = gates[:, :D_MODEL] * ypool_ref[0].astype(f32) + gates[:, D_MODEL:] * y_ssd
    h1 = h_ref[0] + _dot(mix.astype(bf16), wo_ref[...])
    ms = jnp.mean(h1 * h1, axis=-1, keepdims=True)
    v = (h1 * jax.lax.rsqrt(ms + EPS) * mw_ref[...]).astype(bf16)
    acc = h1
    for k in range(D_FF // FF_CHUNK):
        hid = jnp.maximum(_dot(v, ff1_ref[:, k * FF_CHUNK:(k + 1) * FF_CHUNK]), 0.0)
        acc = acc + _dot((hid * hid).astype(bf16), ff2_ref[k * FF_CHUNK:(k + 1) * FF_CHUNK, :])
    o_ref[0] = acc


def _merge_mlp(layer, h, proj, ypool, ynorm, bgate, wssd, wo, mw, ff1, ff2):
    bsz, lp, _ = h.shape

    def const(shape):
        return pl.BlockSpec((None,) + shape, lambda b, i: (layer,) + (0,) * len(shape),
                            pipeline_mode=pl.Buffered(1))

    return pl.pallas_call(
        _merge_mlp_kernel,
        grid=(bsz, lp // TM_MLP),
        in_specs=[
            pl.BlockSpec((1, TM_MLP, D_MODEL), lambda b, i: (b, i, 0)),
            pl.BlockSpec((1, TM_MLP, 2 * D_MODEL), lambda b, i: (b, i, COL_GATE // (2 * D_MODEL))),
            pl.BlockSpec((1, TM_MLP, D_MODEL), lambda b, i: (b, i, 0)),
            pl.BlockSpec((1, TM_MLP, D_INNER), lambda b, i: (b, i, 0)),
            const((1, 2 * D_MODEL)),
            const((D_INNER, D_MODEL)),
            const((D_MODEL, D_MODEL)),
            const((1, D_MODEL)),
            const((D_MODEL, D_FF)),
            const((D_FF, D_MODEL)),
        ],
        out_specs=pl.BlockSpec((1, TM_MLP, D_MODEL), lambda b, i: (b, i, 0)),
        out_shape=jax.ShapeDtypeStruct((bsz, lp, D_MODEL), f32),
        compiler_params=pltpu.CompilerParams(
            dimension_semantics=("parallel", "parallel"),
            vmem_limit_bytes=VMEM_LIMIT),
        name="merge_mlp",
    )(h, proj, ypool, ynorm, bgate, wssd, wo, mw, ff1, ff2)


def _final_norm_kernel(h_ref, w_ref, o_ref):
    x = h_ref[0]
    ms = jnp.mean(x * x, axis=-1, keepdims=True)
    o_ref[0] = x * jax.lax.rsqrt(ms + EPS) * w_ref[...]


def _final_norm(h, w, seq):
    bsz = h.shape[0]
    front = h.shape[1] - seq
    return pl.pallas_call(
        _final_norm_kernel,
        grid=(bsz, seq // TM_FINAL),
        in_specs=[
            pl.BlockSpec((pl.Element(1), pl.Element(TM_FINAL), pl.Element(D_MODEL)),
                         lambda b, i: (b, pl.multiple_of(front + i * TM_FINAL, CHUNK), 0)),
            pl.BlockSpec((1, D_MODEL), lambda b, i: (0, 0)),
        ],
        out_specs=pl.BlockSpec((1, TM_FINAL, D_MODEL), lambda b, i: (b, i, 0)),
        out_shape=jax.ShapeDtypeStruct((bsz, seq, D_MODEL), f32),
        compiler_params=pltpu.CompilerParams(dimension_semantics=("parallel", "parallel")),
        name="final_norm",
    )(h, w)


def kernel(x, meta_tokens, mix_norm_w, w_in, b_gate, pool_w_group, pool_scale, w_pool_up,
           conv_w, conv_b, dt_bias, a_log, d_skip, ssd_norm_w, w_ssd_out, w_o,
           mlp_norm_w, w_ff1, w_ff2, final_norm_w):
    bsz, seq, _ = x.shape
    meta = jnp.broadcast_to(meta_tokens[None].astype(x.dtype), (bsz, N_META, D_MODEL))
    h = jnp.concatenate([jnp.zeros((bsz, PAD_FRONT, D_MODEL), x.dtype), meta, x], axis=1)

    w_main = jnp.concatenate([
        w_in[:, :, OFF_XBC:OFF_XBC + D_INNER],
        w_in[:, :, OFF_Z:OFF_XBC] * 0.5,
        w_in[:, :, OFF_GATE:],
        w_in[:, :, OFF_XBC + D_INNER:OFF_DT],
        w_in[:, :, OFF_POOL:OFF_Z],
    ], axis=-1).astype(bf16)
    w_dt = jnp.pad(w_in[:, :, OFF_DT:OFF_GATE], ((0, 0), (0, 0), (0, DT_LANES - SSM_HEADS))).astype(bf16)
    row = lambda p: p[:, None, :]
    dtb = row(jnp.pad(dt_bias, ((0, 0), (0, DT_LANES - SSM_HEADS))))
    dskip = row(jnp.repeat(d_skip, SSM_HEAD_DIM, axis=-1))
    eexp = (jnp.arange(DT_LANES)[:, None] == (jnp.arange(D_INNER)[None, :] // SSM_HEAD_DIM)).astype(bf16)
    cw = conv_w * 0.5
    cb = row(conv_b * 0.5)
    alog = a_log[:, :, None]
    wg = pool_w_group.astype(bf16)
    wup = w_pool_up.astype(bf16)
    wssd = w_ssd_out.astype(bf16)
    wo = w_o.astype(bf16)
    ff1 = w_ff1.astype(bf16)
    ff2 = w_ff2.astype(bf16)
    mix_nw, pscale, ssd_nw = row(mix_norm_w), row(pool_scale), row(ssd_norm_w)
    bgate, mlp_nw = row(b_gate), row(mlp_norm_w)

    for i in range(DEPTH):
        proj, dt_raw = _in_proj(i, h, mix_nw, w_main, w_dt)
        ypool = _pool(i, proj, wg, pscale, wup)
        ynorm = _ssd(i, proj, dt_raw, cw, cb, dtb, alog, dskip, ssd_nw, eexp)
        h = _merge_mlp(i, h, proj, ypool, ynorm, bgate, wssd, wo, mlp_nw, ff1, ff2)
    return _final_norm(h, final_norm_w[None], seq)
```

```python
import math

import jax
import jax.numpy as jnp
from jax.experimental import pallas as pl
from jax.experimental.pallas import tpu as pltpu

D_MODEL = 1024
DEPTH = 4
N_META = 16
POOL_GROUPS = 4
POOL_GROUP_DIM = 256
POOL_WINDOWS = (2, 4, 8, 16)
D_INNER = 2048
SSM_HEAD_DIM = 64
SSM_HEADS = 32
SSM_GROUPS = 8
HEADS_PER_GROUP = 4
D_STATE = 128
CONV_WIDTH = 4
CHUNK = 128
D_FF = 4096
EPS = 1e-5
LOG2E = math.log2(math.e)

OFF_DT = 7168
OFF_GATE = 7200
W_BLOCK = 1024
WBLK_POOL, WBLK_Z, WBLK_XS, WBLK_B, WBLK_C = 0, 1, 3, 5, 6

COL_Z = 0
COL_XS = 2048
COL_GATE = 4096
COL_POOL = 6144
COL_B = 7168
COL_C = 8192
N_MAIN = 9216
DT_LANES = 128

PAD_FRONT = CHUNK - N_META
SUBLANES = 8
LANES = 128
REGS = CHUNK // SUBLANES
CONV_HALO = 32
GROUP_X = HEADS_PER_GROUP * SSM_HEAD_DIM
EARLY_CONV_GROUPS = 2

TM_PROJ = 1408
TM_POOL = 384
TM_MLP = 384
TM_FINAL = 512
FF_CHUNK = 1024
VMEM_LIMIT = 56 * 1024 * 1024

f32 = jnp.float32
bf16 = jnp.bfloat16


def _dot(a, b):
    return jnp.dot(a, b, preferred_element_type=f32)


def _token_of(q):
    return jnp.right_shift(q, 3) + jnp.bitwise_and(q, SUBLANES - 1) * REGS


def _split3(x):
    p1 = x.astype(bf16)
    r1 = x - p1.astype(f32)
    p2 = r1.astype(bf16)
    p3 = (r1 - p2.astype(f32)).astype(bf16)
    return p1, p2, p3


def _dot_exact_rhs(x, e):
    p1, p2, p3 = _split3(x)
    return _dot(p1, e) + _dot(p2, e) + _dot(p3, e)


def _sigmoid(x):
    return 1.0 / (1.0 + jnp.exp(-x))


def _silu_of_twice(hx):
    return hx + hx * jnp.tanh(hx)


def _softplus(x):
    return jnp.maximum(x, 0.0) + jnp.log1p(jnp.exp(-jnp.abs(x)))


GATE_BLK0 = COL_GATE // W_BLOCK


def _in_proj_kernel(h_ref, nw_ref, w_ref, wgate_ref, wdt_ref, o_ref, dt_ref, u_scr):
    i = pl.program_id(1)
    j = pl.program_id(2)

    @pl.when(j == 0)
    def _():
        x = h_ref[0]
        ms = jnp.mean(x * x, axis=-1, keepdims=True)
        u = x * jax.lax.rsqrt(ms + EPS) * nw_ref[...]
        r = i * TM_PROJ + jax.lax.broadcasted_iota(jnp.int32, (TM_PROJ, 1), 0)
        valid = jnp.logical_or(r >= CHUNK, jnp.bitwise_and(r, SUBLANES - 1) == SUBLANES - 1)
        u = jnp.where(valid, u, 0.0).astype(bf16)
        u_scr[...] = u
        dt_ref[0] = _dot(u, wdt_ref[...])

    is_gate = jnp.logical_and(j >= GATE_BLK0, j < GATE_BLK0 + 2)

    @pl.when(is_gate)
    def _():
        o_ref[0] = _dot(u_scr[...], wgate_ref[...]).astype(bf16)

    @pl.when(jnp.logical_not(is_gate))
    def _():
        o_ref[0] = _dot(u_scr[...], w_ref[...].astype(bf16)).astype(bf16)


def _w_block_of(j):
    after_gate = jnp.where(j == COL_POOL // W_BLOCK, WBLK_POOL, j - 2)
    return jnp.where(j < GATE_BLK0, j + WBLK_Z, jnp.where(j < GATE_BLK0 + 2, WBLK_XS + 1, after_gate))


def _in_proj(layer, h, nw, w_in, w_gate, w_dt):
    bsz, lp, _ = h.shape
    grid = (bsz, lp // TM_PROJ, N_MAIN // W_BLOCK)
    return pl.pallas_call(
        _in_proj_kernel,
        grid=grid,
        in_specs=[
            pl.BlockSpec((1, TM_PROJ, D_MODEL), lambda b, i, j: (b, i, 0)),
            pl.BlockSpec((None, 1, D_MODEL), lambda b, i, j: (layer, 0, 0)),
            pl.BlockSpec((None, D_MODEL, W_BLOCK), lambda b, i, j: (layer, 0, _w_block_of(j))),
            pl.BlockSpec((None, D_MODEL, W_BLOCK),
                         lambda b, i, j: (layer, 0, jnp.clip(j - GATE_BLK0, 0, 1))),
            pl.BlockSpec((None, D_MODEL, DT_LANES), lambda b, i, j: (layer, 0, 0)),
        ],
        out_specs=[
            pl.BlockSpec((1, TM_PROJ, W_BLOCK), lambda b, i, j: (b, i, j)),
            pl.BlockSpec((1, TM_PROJ, DT_LANES), lambda b, i, j: (b, i, 0)),
        ],
        out_shape=[
            jax.ShapeDtypeStruct((bsz, lp, N_MAIN), bf16),
            jax.ShapeDtypeStruct((bsz, lp, DT_LANES), f32),
        ],
        scratch_shapes=[pltpu.VMEM((TM_PROJ, D_MODEL), bf16)],
        compiler_params=pltpu.CompilerParams(
            dimension_semantics=("parallel", "parallel", "arbitrary"),
            vmem_limit_bytes=VMEM_LIMIT),
        name="in_proj",
    )(h, nw, w_in, w_gate, w_dt)


def _pool_kernel(u_ref, prev_ref, wg_ref, scale_ref, wup_ref, o_ref, pooled_scr, y1_scr):
    i = pl.program_id(1)
    prev_chunk = prev_ref[0]
    prev_chunk = jnp.where(i > 0, prev_chunk, jnp.zeros_like(prev_chunk))

    q_io = jax.lax.broadcasted_iota(jnp.int32, (CHUNK, 2 * CHUNK), 0)
    k_io = jax.lax.broadcasted_iota(jnp.int32, (CHUNK, 2 * CHUNK), 1)
    k_tok = _token_of(jnp.bitwise_and(k_io, CHUNK - 1)) + jnp.right_shift(k_io, 7) * CHUNK
    delta = _token_of(q_io) + CHUNK - k_tok
    tok_in_chunk = _token_of(jax.lax.broadcasted_iota(jnp.int32, (CHUNK, 1), 0))
    for g, win in enumerate(POOL_WINDOWS):
        band = jnp.where(jnp.logical_and(delta >= 0, delta < win), 1.0, 0.0).astype(bf16)
        cols = slice(g * POOL_GROUP_DIM, (g + 1) * POOL_GROUP_DIM)
        for sb in range(TM_POOL // CHUNK):
            rows = slice(sb * CHUNK, (sb + 1) * CHUNK)
            before = prev_chunk[:, cols] if sb == 0 else u_ref[0, (sb - 1) * CHUNK:sb * CHUNK, cols]
            cur = u_ref[0, rows, cols]
            wsum = _dot(band, jnp.concatenate([before, cur], axis=0))
            tok = i * TM_POOL + sb * CHUNK + tok_in_chunk - PAD_FRONT
            cnt = jnp.clip(tok + 1, 1, win).astype(f32)
            pooled_scr[rows, cols] = (wsum / cnt - cur.astype(f32)).astype(bf16)
        y1 = _dot(pooled_scr[:, cols], wg_ref[g]) * scale_ref[:, cols]
        y1_scr[:, cols] = y1.astype(bf16)
    o_ref[0] = _dot(y1_scr[...], wup_ref[...]).astype(bf16)


def _pool(layer, proj, wg, scale, wup):
    bsz, lp, _ = proj.shape
    rb = TM_POOL // CHUNK
    return pl.pallas_call(
        _pool_kernel,
        grid=(bsz, lp // TM_POOL),
        in_specs=[
            pl.BlockSpec((1, TM_POOL, D_MODEL), lambda b, i: (b, i, COL_POOL // D_MODEL)),
            pl.BlockSpec((1, CHUNK, D_MODEL),
                         lambda b, i: (b, jnp.maximum(i * rb - 1, 0), COL_POOL // D_MODEL)),
            pl.BlockSpec((None, POOL_GROUPS, POOL_GROUP_DIM, POOL_GROUP_DIM),
                         lambda b, i: (layer, 0, 0, 0)),
            pl.BlockSpec((None, 1, D_MODEL), lambda b, i: (layer, 0, 0)),
            pl.BlockSpec((None, D_MODEL, D_MODEL), lambda b, i: (layer, 0, 0)),
        ],
        out_specs=pl.BlockSpec((1, TM_POOL, D_MODEL), lambda b, i: (b, i, 0)),
        out_shape=jax.ShapeDtypeStruct((bsz, lp, D_MODEL), bf16),
        scratch_shapes=[
            pltpu.VMEM((TM_POOL, D_MODEL), bf16),
            pltpu.VMEM((TM_POOL, D_MODEL), bf16),
        ],
        compiler_params=pltpu.CompilerParams(
            dimension_semantics=("parallel", "parallel"),
            vmem_limit_bytes=VMEM_LIMIT),
        name="pool",
    )(proj, proj, wg, scale, wup)


def _conv_silu(halo, cur, w, b, use_halo):
    n = cur.shape[1]
    tail = CONV_WIDTH - 1
    x3 = cur.astype(f32).reshape(REGS, SUBLANES, n)
    prev_tail = halo[CONV_HALO - tail * SUBLANES:, :].astype(f32)
    prev_tail = jnp.where(use_halo, prev_tail, 0.0).reshape(tail, SUBLANES, n)
    merged = jnp.concatenate([x3[REGS - tail:, :SUBLANES - 1, :], prev_tail[:, SUBLANES - 1:, :]], axis=1)
    wrapped = pltpu.roll(merged, 1, axis=1)
    acc = b.reshape(1, 1, n) + x3 * w[CONV_WIDTH - 1:CONV_WIDTH, :].reshape(1, 1, n)
    for d in range(1, CONV_WIDTH):
        shifted = jnp.concatenate([wrapped[tail - d:], x3[:REGS - d]], axis=0)
        acc = acc + shifted * w[CONV_WIDTH - 1 - d:CONV_WIDTH - d, :].reshape(1, 1, n)
    return _silu_of_twice(acc).reshape(CHUNK, n)


def _ssd_kernel(xs_ref, xs_h_ref, b_ref, b_h_ref, c_ref, c_h_ref, z_ref, dt_ref,
                cw_ref, cb_ref, dtb_ref, alog_ref, dskip_ref, nw_ref, eexp_ref,
                o_ref, state_scr):
    c = pl.program_id(1)

    @pl.when(c == 0)
    def _():
        state_scr[...] = jnp.zeros_like(state_scr)

    use_halo = c > 0

    def conv_group(g):
        xcols = slice(g * GROUP_X, (g + 1) * GROUP_X)
        ncols = slice(g * D_STATE, (g + 1) * D_STATE)
        bc_b = slice(D_INNER + g * D_STATE, D_INNER + (g + 1) * D_STATE)
        bc_c = slice(D_INNER + SSM_GROUPS * D_STATE + g * D_STATE,
                     D_INNER + SSM_GROUPS * D_STATE + (g + 1) * D_STATE)
        xg = _conv_silu(xs_h_ref[0, :, xcols], xs_ref[0, :, xcols],
                        cw_ref[:, xcols], cb_ref[:, xcols], use_halo)
        bg = _conv_silu(b_h_ref[0, :, ncols], b_ref[0, :, ncols],
                        cw_ref[:, bc_b], cb_ref[:, bc_b], use_halo)
        cg = _conv_silu(c_h_ref[0, :, ncols], c_ref[0, :, ncols],
                        cw_ref[:, bc_c], cb_ref[:, bc_c], use_halo)
        return xg, bg.astype(bf16), cg

    early_convs = [conv_group(g) for g in range(EARLY_CONV_GROUPS)]

    tok = _token_of(jax.lax.broadcasted_iota(jnp.int32, (1, CHUNK), 1))
    valid_t = jnp.logical_or(c > 0, tok >= PAD_FRONT)
    raw_t = (dt_ref[0] + dtb_ref[...]).T[0:SSM_HEADS, :]
    dt_t = jnp.where(valid_t, _softplus(raw_t), 0.0)
    a_col = -jnp.exp(alog_ref[...])
    tok_r = _token_of(jax.lax.broadcasted_iota(jnp.int32, (CHUNK, CHUNK), 0))
    tok_c = _token_of(jax.lax.broadcasted_iota(jnp.int32, (CHUNK, CHUNK), 1))
    causal = tok_r >= tok_c
    tri_t = jnp.where(tok_r <= tok_c, 1.0, 0.0).astype(bf16)
    acs_t = _dot_exact_rhs(dt_t * a_col, tri_t)
    acs_last = acs_t[:, CHUNK - 1:CHUNK]
    rowarg_t = acs_t * LOG2E - jnp.log(dt_t) * LOG2E
    w_state_t = jnp.exp(acs_last - acs_t) * dt_t
    e_acs_last = jnp.exp(acs_last)

    def tokens_on_rows(x):
        pad = jnp.zeros((CHUNK - SSM_HEADS, CHUNK), f32)
        return jnp.concatenate([x, pad], axis=0).T

    acs2 = tokens_on_rows(acs_t * LOG2E)
    w_exp = _dot(tokens_on_rows(w_state_t).astype(bf16), eexp_ref[...])
    chunk_decay = tokens_on_rows(jnp.broadcast_to(e_acs_last, (SSM_HEADS, CHUNK)))[0:SUBLANES, :]
    cd_exp = _dot_exact_rhs(chunk_decay, eexp_ref[...])[0:1, :]

    low_head = jax.lax.broadcasted_iota(jnp.int32, (1, LANES), 1) < SSM_HEAD_DIM

    for g in range(SSM_GROUPS):
        xcols = slice(g * GROUP_X, (g + 1) * GROUP_X)
        xg, bg, cg = early_convs[g] if g < EARLY_CONV_GROUPS else conv_group(g)
        cg_b = cg.astype(bf16)
        xg_b = xg.astype(bf16)
        s_old = state_scr[g]
        s_b = s_old.astype(bf16)

        cb = jax.lax.dot_general(cg_b, bg, (((1,), (1,)), ((), ())), preferred_element_type=f32)

        halves = []
        for half in range(HEADS_PER_GROUP // 2):
            hcols = slice(half * LANES, (half + 1) * LANES)
            rhs = jnp.concatenate([xg_b[:, hcols], s_b[:, hcols]], axis=0)
            outs = []
            for r in (2 * half, 2 * half + 1):
                h = g * HEADS_PER_GROUP + r
                col = jnp.broadcast_to(acs2[:, h:h + 1], (CHUNK, CHUNK))
                lm = jnp.exp2(col - rowarg_t[h:h + 1, :])
                m = (jnp.where(causal, lm, 0.0) * cb).astype(bf16)
                ce = (cg * jnp.exp2(col)).astype(bf16)
                outs.append(_dot(jnp.concatenate([m, ce], axis=1), rhs))
            halves.append(jnp.where(low_head, outs[0], outs[1]))
        y = jnp.concatenate(halves, axis=1)

        xw = (xg * w_exp[:, xcols]).astype(bf16)
        st = jax.lax.dot_general(bg, xw, (((0,), (0,)), ((), ())), preferred_element_type=f32)
        state_scr[g] = s_old * cd_exp[:, xcols] + st

        y = y + xg * dskip_ref[:, xcols]
        y = y * _silu_of_twice(0.5 * z_ref[0, :, xcols].astype(f32))
        ms = jnp.mean(y * y, axis=-1, keepdims=True)
        y = y * jax.lax.rsqrt(ms + EPS) * nw_ref[:, xcols]
        o_ref[0, :, xcols] = y.astype(bf16)


def _ssd(layer, proj, dt_raw, cw, cb, dtb, alog, dskip, nw, eexp):
    bsz, lp, _ = proj.shape
    rb = CHUNK // CONV_HALO

    def cur(width, col):
        return pl.BlockSpec((1, CHUNK, width), lambda b, c: (b, c, col // width))

    def halo(width, col):
        return pl.BlockSpec((1, CONV_HALO, width),
                            lambda b, c: (b, jnp.maximum(c * rb - 1, 0), col // width))

    def const(shape):
        return pl.BlockSpec((None,) + shape, lambda b, c: (layer,) + (0,) * len(shape))

    n_bc = SSM_GROUPS * D_STATE
    return pl.pallas_call(
        _ssd_kernel,
        grid=(bsz, lp // CHUNK),
        in_specs=[
            cur(D_INNER, COL_XS), halo(D_INNER, COL_XS),
            cur(n_bc, COL_B), halo(n_bc, COL_B),
            cur(n_bc, COL_C), halo(n_bc, COL_C),
            cur(D_INNER, COL_Z),
            pl.BlockSpec((1, CHUNK, DT_LANES), lambda b, c: (b, c, 0)),
            const((CONV_WIDTH, D_INNER + 2 * n_bc)),
            const((1, D_INNER + 2 * n_bc)),
            const((1, DT_LANES)),
            const((SSM_HEADS, 1)),
            const((1, D_INNER)),
            const((1, D_INNER)),
            pl.BlockSpec((DT_LANES, D_INNER), lambda b, c: (0, 0)),
        ],
        out_specs=pl.BlockSpec((1, CHUNK, D_INNER), lambda b, c: (b, c, 0)),
        out_shape=jax.ShapeDtypeStruct((bsz, lp, D_INNER), bf16),
        scratch_shapes=[
            pltpu.VMEM((SSM_GROUPS, D_STATE, GROUP_X), f32),
        ],
        compiler_params=pltpu.CompilerParams(
            dimension_semantics=("parallel", "arbitrary"),
            vmem_limit_bytes=VMEM_LIMIT),
        name="ssd",
    )(proj, proj, proj, proj, proj, proj, proj, dt_raw, cw, cb, dtb, alog, dskip, nw, eexp)


def _merge_mlp_kernel(h_ref, gate_ref, ypool_ref, ynorm_ref, bgate_ref, wssd_ref, wo_ref,
                      mw_ref, ff1_ref, ff2_ref, o_ref):
    y_ssd = _dot(ynorm_ref[0], wssd_ref[...])
    gates = _sigmoid(gate_ref[0].astype(f32) + bgate_ref[...])
    mix = gates[:, :D_MODEL] * ypool_ref[0].astype(f32) + gates[:, D_MODEL:] * y_ssd
    h1 = h_ref[0] + _dot(mix.astype(bf16), wo_ref[...])
    ms = jnp.mean(h1 * h1, axis=-1, keepdims=True)
    v = (h1 * jax.lax.rsqrt(ms + EPS) * mw_ref[...]).astype(bf16)
    acc = h1
    for k in range(D_FF // FF_CHUNK):
        hid = jnp.maximum(_dot(v, ff1_ref[:, k * FF_CHUNK:(k + 1) * FF_CHUNK]), 0.0)
        acc = acc + _dot((hid * hid).astype(bf16), ff2_ref[k * FF_CHUNK:(k + 1) * FF_CHUNK, :])
    o_ref[0] = acc


def _merge_mlp(layer, h, proj, ypool, ynorm, bgate, wssd, wo, mw, ff1, ff2):
    bsz, lp, _ = h.shape

    def const(shape):
        return pl.BlockSpec((None,) + shape, lambda b, i: (layer,) + (0,) * len(shape),
                            pipeline_mode=pl.Buffered(1))

    return pl.pallas_call(
        _merge_mlp_kernel,
        grid=(bsz, lp // TM_MLP),
        in_specs=[
            pl.BlockSpec((1, TM_MLP, D_MODEL), lambda b, i: (b, i, 0)),
            pl.BlockSpec((1, TM_MLP, 2 * D_MODEL), lambda b, i: (b, i, COL_GATE // (2 * D_MODEL))),
            pl.BlockSpec((1, TM_MLP, D_MODEL), lambda b, i: (b, i, 0)),
            pl.BlockSpec((1, TM_MLP, D_INNER), lambda b, i: (b, i, 0)),
            const((1, 2 * D_MODEL)),
            const((D_INNER, D_MODEL)),
            const((D_MODEL, D_MODEL)),
            const((1, D_MODEL)),
            const((D_MODEL, D_FF)),
            const((D_FF, D_MODEL)),
        ],
        out_specs=pl.BlockSpec((1, TM_MLP, D_MODEL), lambda b, i: (b, i, 0)),
        out_shape=jax.ShapeDtypeStruct((bsz, lp, D_MODEL), f32),
        compiler_params=pltpu.CompilerParams(
            dimension_semantics=("parallel", "parallel"),
            vmem_limit_bytes=VMEM_LIMIT),
        name="merge_mlp",
    )(h, proj, ypool, ynorm, bgate, wssd, wo, mw, ff1, ff2)


def _final_norm_kernel(h_ref, w_ref, o_ref):
    x = h_ref[0]
    ms = jnp.mean(x * x, axis=-1, keepdims=True)
    o_ref[0] = x * jax.lax.rsqrt(ms + EPS) * w_ref[...]


def _final_norm(h, w, seq):
    bsz = h.shape[0]
    front = h.shape[1] - seq
    return pl.pallas_call(
        _final_norm_kernel,
        grid=(bsz, seq // TM_FINAL),
        in_specs=[
            pl.BlockSpec((pl.Element(1), pl.Element(TM_FINAL), pl.Element(D_MODEL)),
                         lambda b, i: (b, pl.multiple_of(front + i * TM_FINAL, CHUNK), 0)),
            pl.BlockSpec((1, D_MODEL), lambda b, i: (0, 0)),
        ],
        out_specs=pl.BlockSpec((1, TM_FINAL, D_MODEL), lambda b, i: (b, i, 0)),
        out_shape=jax.ShapeDtypeStruct((bsz, seq, D_MODEL), f32),
        compiler_params=pltpu.CompilerParams(dimension_semantics=("parallel", "parallel")),
        name="final_norm",
    )(h, w)


def _to_strided_rows(t):
    bsz, n, d = t.shape
    t = t.reshape(bsz, n // CHUNK, SUBLANES, REGS, d)
    return jnp.swapaxes(t, 2, 3).reshape(bsz, n, d)


def _from_strided_rows(t):
    bsz, n, d = t.shape
    t = t.reshape(bsz, n // CHUNK, REGS, SUBLANES, d)
    return jnp.swapaxes(t, 2, 3).reshape(bsz, n, d)


def kernel(x, meta_tokens, mix_norm_w, w_in, b_gate, pool_w_group, pool_scale, w_pool_up,
           conv_w, conv_b, dt_bias, a_log, d_skip, ssd_norm_w, w_ssd_out, w_o,
           mlp_norm_w, w_ff1, w_ff2, final_norm_w):
    bsz, seq, _ = x.shape
    meta = jnp.broadcast_to(meta_tokens[None].astype(x.dtype), (bsz, N_META, D_MODEL))
    first_chunk = jnp.concatenate([jnp.zeros((bsz, PAD_FRONT, D_MODEL), x.dtype), meta], axis=1)
    h = _to_strided_rows(jnp.concatenate([first_chunk, x], axis=1))

    w_gate = w_in[:, :, OFF_GATE:].astype(bf16)
    w_dt = jnp.pad(w_in[:, :, OFF_DT:OFF_GATE], ((0, 0), (0, 0), (0, DT_LANES - SSM_HEADS))).astype(bf16)
    row = lambda p: p[:, None, :]
    dtb = row(jnp.pad(dt_bias, ((0, 0), (0, DT_LANES - SSM_HEADS))))
    dskip = row(jnp.repeat(d_skip, SSM_HEAD_DIM, axis=-1))
    eexp = (jnp.arange(DT_LANES)[:, None] == (jnp.arange(D_INNER)[None, :] // SSM_HEAD_DIM)).astype(bf16)
    cw = conv_w * 0.5
    cb = row(conv_b * 0.5)
    alog = a_log[:, :, None]
    wg = pool_w_group.astype(bf16)
    wup = w_pool_up.astype(bf16)
    wssd = w_ssd_out.astype(bf16)
    wo = w_o.astype(bf16)
    ff1 = w_ff1.astype(bf16)
    ff2 = w_ff2.astype(bf16)
    mix_nw, pscale, ssd_nw = row(mix_norm_w), row(pool_scale), row(ssd_norm_w)
    bgate, mlp_nw = row(b_gate), row(mlp_norm_w)

    for i in range(DEPTH):
        proj, dt_raw = _in_proj(i, h, mix_nw, w_in, w_gate, w_dt)
        ypool = _pool(i, proj, wg, pscale, wup)
        ynorm = _ssd(i, proj, dt_raw, cw, cb, dtb, alog, dskip, ssd_nw, eexp)
        h = _merge_mlp(i, h, proj, ypool, ynorm, bgate, wssd, wo, mlp_nw, ff1, ff2)
    return _from_strided_rows(_final_norm(h, final_norm_w[None], seq))
```

```python
import functools
import math

import jax
import jax.numpy as jnp
from jax.experimental import pallas as pl
from jax.experimental.pallas import tpu as pltpu

D_MODEL = 1024
DEPTH = 4
N_META = 16
POOL_GROUPS = 4
POOL_GROUP_DIM = 256
POOL_WINDOWS = (2, 4, 8, 16)
D_INNER = 2048
SSM_HEAD_DIM = 64
SSM_HEADS = 32
SSM_GROUPS = 8
HEADS_PER_GROUP = 4
D_STATE = 128
CONV_WIDTH = 4
CHUNK = 128
D_FF = 4096
EPS = 1e-5
LOG2E = math.log2(math.e)

OFF_DT = 7168
OFF_GATE = 7200
W_BLOCK = 1024
WBLK_POOL, WBLK_Z, WBLK_XS, WBLK_B, WBLK_C = 0, 1, 3, 5, 6

COL_Z = 0
COL_XS = 2048
COL_GATE = 4096
COL_POOL = 6144
COL_B = 7168
COL_C = 8192
N_MAIN = 9216
DT_LANES = 128

PAD_FRONT = CHUNK - N_META
SUBLANES = 8
LANES = 128
REGS = CHUNK // SUBLANES
CONV_HALO = 32
GROUP_X = HEADS_PER_GROUP * SSM_HEAD_DIM
EARLY_CONV_GROUPS = 2

TM_PROJ = 1408
TM_POOL = 384
TM_MLP = 384
FF_CHUNK = 1024
VMEM_LIMIT = 56 * 1024 * 1024

f32 = jnp.float32
bf16 = jnp.bfloat16


def _dot(a, b):
    return jnp.dot(a, b, preferred_element_type=f32)


def _token_of(q):
    return jnp.right_shift(q, 3) + jnp.bitwise_and(q, SUBLANES - 1) * REGS


def _split3(x):
    p1 = x.astype(bf16)
    r1 = x - p1.astype(f32)
    p2 = r1.astype(bf16)
    p3 = (r1 - p2.astype(f32)).astype(bf16)
    return p1, p2, p3


def _dot_exact_rhs(x, e):
    p1, p2, p3 = _split3(x)
    return _dot(p1, e) + _dot(p2, e) + _dot(p3, e)


def _sigmoid(x):
    return 1.0 / (1.0 + jnp.exp(-x))


def _silu_of_twice(hx):
    return hx + hx * jnp.tanh(hx)


def _softplus(x):
    return jnp.maximum(x, 0.0) + jnp.log1p(jnp.exp(-jnp.abs(x)))


GATE_BLK0 = COL_GATE // W_BLOCK
NT_DIMS = (((1,), (1,)), ((), ()))


def _in_proj_kernel(h_ref, nw_ref, w_ref, wdt_ref, o_ref, dt_ref, u_scr):
    j = pl.program_id(1)
    i = pl.program_id(2)
    rows = pl.ds(pl.multiple_of(i * TM_PROJ, CHUNK), TM_PROJ)

    @pl.when(j == 0)
    def _():
        x = h_ref[0]
        ms = jnp.mean(x * x, axis=-1, keepdims=True)
        u = x * jax.lax.rsqrt(ms + EPS) * nw_ref[...]
        r = i * TM_PROJ + jax.lax.broadcasted_iota(jnp.int32, (TM_PROJ, 1), 0)
        valid = jnp.logical_or(r >= CHUNK, jnp.bitwise_and(r, SUBLANES - 1) == SUBLANES - 1)
        u = jnp.where(valid, u, 0.0).astype(bf16)
        u_scr[rows, :] = u
        dt_ref[0] = jax.lax.dot_general(u, wdt_ref[0].astype(bf16), NT_DIMS, preferred_element_type=f32)

    w = w_ref[0].astype(bf16)
    acc = jax.lax.dot_general(u_scr[rows, :], w, NT_DIMS, preferred_element_type=f32)
    o_ref[0] = (acc * jnp.where(j < COL_XS // W_BLOCK, 0.5, 1.0)).astype(bf16)


def _w_row_of(j):
    blk = jnp.where(j < GATE_BLK0, j + WBLK_Z, jnp.where(j == COL_POOL // W_BLOCK, WBLK_POOL, j - 2))
    is_gate = jnp.logical_and(j >= GATE_BLK0, j < GATE_BLK0 + 2)
    return pl.multiple_of(jnp.where(is_gate, OFF_GATE + (j - GATE_BLK0) * W_BLOCK, blk * W_BLOCK), SUBLANES)


def _in_proj(layer, h, nw, w_t):
    bsz, lp, _ = h.shape
    n_i = lp // TM_PROJ
    grid = (bsz, N_MAIN // W_BLOCK, n_i)

    def first_pass_tile(b, j, i):
        return (b, jnp.where(j == 0, i, n_i - 1), 0)

    elem = lambda *shape: tuple(pl.Element(s) for s in shape)
    return pl.pallas_call(
        _in_proj_kernel,
        grid=grid,
        in_specs=[
            pl.BlockSpec((1, TM_PROJ, D_MODEL), first_pass_tile),
            pl.BlockSpec((None, 1, D_MODEL), lambda b, j, i: (layer, 0, 0)),
            pl.BlockSpec(elem(1, W_BLOCK, D_MODEL), lambda b, j, i: (layer, _w_row_of(j), 0)),
            pl.BlockSpec(elem(1, DT_LANES, D_MODEL), lambda b, j, i: (layer, OFF_DT, 0)),
        ],
        out_specs=[
            pl.BlockSpec((1, TM_PROJ, W_BLOCK), lambda b, j, i: (b, i, j)),
            pl.BlockSpec((1, TM_PROJ, DT_LANES), first_pass_tile),
        ],
        out_shape=[
            jax.ShapeDtypeStruct((bsz, lp, N_MAIN), bf16),
            jax.ShapeDtypeStruct((bsz, lp, DT_LANES), f32),
        ],
        scratch_shapes=[pltpu.VMEM((lp, D_MODEL), bf16)],
        compiler_params=pltpu.CompilerParams(
            dimension_semantics=("parallel", "arbitrary", "arbitrary"),
            vmem_limit_bytes=VMEM_LIMIT),
        name="in_proj",
    )(h, nw, w_t, w_t)


def _pool_kernel(u_ref, prev_ref, wg_ref, scale_ref, wup_ref, o_ref, pooled_scr, y1_scr):
    i = pl.program_id(1)
    prev_chunk = prev_ref[0]
    prev_chunk = jnp.where(i > 0, prev_chunk, jnp.zeros_like(prev_chunk))

    q_io = jax.lax.broadcasted_iota(jnp.int32, (CHUNK, 2 * CHUNK), 0)
    k_io = jax.lax.broadcasted_iota(jnp.int32, (CHUNK, 2 * CHUNK), 1)
    k_tok = _token_of(jnp.bitwise_and(k_io, CHUNK - 1)) + jnp.right_shift(k_io, 7) * CHUNK
    delta = _token_of(q_io) + CHUNK - k_tok
    tok_in_chunk = _token_of(jax.lax.broadcasted_iota(jnp.int32, (CHUNK, 1), 0))
    for g, win in enumerate(POOL_WINDOWS):
        band = jnp.where(jnp.logical_and(delta >= 0, delta < win), 1.0, 0.0).astype(bf16)
        cols = slice(g * POOL_GROUP_DIM, (g + 1) * POOL_GROUP_DIM)
        for sb in range(TM_POOL // CHUNK):
            rows = slice(sb * CHUNK, (sb + 1) * CHUNK)
            before = prev_chunk[:, cols] if sb == 0 else u_ref[0, (sb - 1) * CHUNK:sb * CHUNK, cols]
            cur = u_ref[0, rows, cols]
            wsum = _dot(band, jnp.concatenate([before, cur], axis=0))
            tok = i * TM_POOL + sb * CHUNK + tok_in_chunk - PAD_FRONT
            cnt = jnp.clip(tok + 1, 1, win).astype(f32)
            pooled_scr[rows, cols] = (wsum / cnt - cur.astype(f32)).astype(bf16)
        y1 = _dot(pooled_scr[:, cols], wg_ref[g]) * scale_ref[:, cols]
        y1_scr[:, cols] = y1.astype(bf16)
    o_ref[0] = _dot(y1_scr[...], wup_ref[...]).astype(bf16)


def _pool(layer, proj, wg, scale, wup):
    bsz, lp, _ = proj.shape
    rb = TM_POOL // CHUNK
    return pl.pallas_call(
        _pool_kernel,
        grid=(bsz, lp // TM_POOL),
        in_specs=[
            pl.BlockSpec((1, TM_POOL, D_MODEL), lambda b, i: (b, i, COL_POOL // D_MODEL)),
            pl.BlockSpec((1, CHUNK, D_MODEL),
                         lambda b, i: (b, jnp.maximum(i * rb - 1, 0), COL_POOL // D_MODEL)),
            pl.BlockSpec((None, POOL_GROUPS, POOL_GROUP_DIM, POOL_GROUP_DIM),
                         lambda b, i: (layer, 0, 0, 0)),
            pl.BlockSpec((None, 1, D_MODEL), lambda b, i: (layer, 0, 0)),
            pl.BlockSpec((None, D_MODEL, D_MODEL), lambda b, i: (layer, 0, 0)),
        ],
        out_specs=pl.BlockSpec((1, TM_POOL, D_MODEL), lambda b, i: (b, i, 0)),
        out_shape=jax.ShapeDtypeStruct((bsz, lp, D_MODEL), bf16),
        scratch_shapes=[
            pltpu.VMEM((TM_POOL, D_MODEL), bf16),
            pltpu.VMEM((TM_POOL, D_MODEL), bf16),
        ],
        compiler_params=pltpu.CompilerParams(
            dimension_semantics=("parallel", "parallel"),
            vmem_limit_bytes=VMEM_LIMIT),
        name="pool",
    )(proj, proj, wg, scale, wup)


def _conv_silu(halo, cur, w, b, use_halo):
    n = cur.shape[1]
    tail = CONV_WIDTH - 1
    x3 = cur.astype(f32).reshape(REGS, SUBLANES, n)
    prev_tail = halo[CONV_HALO - tail * SUBLANES:, :].astype(f32)
    prev_tail = jnp.where(use_halo, prev_tail, 0.0).reshape(tail, SUBLANES, n)
    merged = jnp.concatenate([x3[REGS - tail:, :SUBLANES - 1, :], prev_tail[:, SUBLANES - 1:, :]], axis=1)
    wrapped = pltpu.roll(merged, 1, axis=1)
    acc = b.reshape(1, 1, n) + x3 * w[CONV_WIDTH - 1:CONV_WIDTH, :].reshape(1, 1, n)
    for d in range(1, CONV_WIDTH):
        shifted = jnp.concatenate([wrapped[tail - d:], x3[:REGS - d]], axis=0)
        acc = acc + shifted * w[CONV_WIDTH - 1 - d:CONV_WIDTH - d, :].reshape(1, 1, n)
    return _silu_of_twice(acc).reshape(CHUNK, n)


def _ssd_kernel(xs_ref, xs_h_ref, b_ref, b_h_ref, c_ref, c_h_ref, z_ref, dt_ref,
                cw_ref, cb_ref, dtb_ref, alog_ref, dskip_ref, nw_ref, eexp_ref,
                o_ref, state_scr):
    c = pl.program_id(1)

    @pl.when(c == 0)
    def _():
        state_scr[...] = jnp.zeros_like(state_scr)

    use_halo = c > 0

    def conv_group(g):
        xcols = slice(g * GROUP_X, (g + 1) * GROUP_X)
        ncols = slice(g * D_STATE, (g + 1) * D_STATE)
        bc_b = slice(D_INNER + g * D_STATE, D_INNER + (g + 1) * D_STATE)
        bc_c = slice(D_INNER + SSM_GROUPS * D_STATE + g * D_STATE,
                     D_INNER + SSM_GROUPS * D_STATE + (g + 1) * D_STATE)
        xg = _conv_silu(xs_h_ref[0, :, xcols], xs_ref[0, :, xcols],
                        cw_ref[:, xcols], cb_ref[:, xcols], use_halo)
        bg = _conv_silu(b_h_ref[0, :, ncols], b_ref[0, :, ncols],
                        cw_ref[:, bc_b], cb_ref[:, bc_b], use_halo)
        cg = _conv_silu(c_h_ref[0, :, ncols], c_ref[0, :, ncols],
                        cw_ref[:, bc_c], cb_ref[:, bc_c], use_halo)
        return xg, bg.astype(bf16), cg

    early_convs = [conv_group(g) for g in range(EARLY_CONV_GROUPS)]

    tok = _token_of(jax.lax.broadcasted_iota(jnp.int32, (1, CHUNK), 1))
    valid_t = jnp.logical_or(c > 0, tok >= PAD_FRONT)
    raw_t = (dt_ref[0] + dtb_ref[...]).T[0:SSM_HEADS, :]
    dt_t = jnp.where(valid_t, _softplus(raw_t), 0.0)
    a_col = -jnp.exp(alog_ref[...])
    tok_r = _token_of(jax.lax.broadcasted_iota(jnp.int32, (CHUNK, CHUNK), 0))
    tok_c = _token_of(jax.lax.broadcasted_iota(jnp.int32, (CHUNK, CHUNK), 1))
    causal = tok_r >= tok_c
    tri_t = jnp.where(tok_r <= tok_c, 1.0, 0.0).astype(bf16)
    acs_t = _dot_exact_rhs(dt_t * a_col, tri_t)
    acs_last = acs_t[:, CHUNK - 1:CHUNK]
    rowarg_t = acs_t * LOG2E - jnp.log(dt_t) * LOG2E
    w_state_t = jnp.exp(acs_last - acs_t) * dt_t
    e_acs_last = jnp.exp(acs_last)

    def tokens_on_rows(x):
        pad = jnp.zeros((CHUNK - SSM_HEADS, CHUNK), f32)
        return jnp.concatenate([x, pad], axis=0).T

    acs2 = tokens_on_rows(acs_t * LOG2E)
    w_exp = _dot(tokens_on_rows(w_state_t).astype(bf16), eexp_ref[...])
    chunk_decay = tokens_on_rows(jnp.broadcast_to(e_acs_last, (SSM_HEADS, CHUNK)))[0:SUBLANES, :]
    cd_exp = _dot_exact_rhs(chunk_decay, eexp_ref[...])[0:1, :]

    low_head = jax.lax.broadcasted_iota(jnp.int32, (1, LANES), 1) < SSM_HEAD_DIM

    for g in range(SSM_GROUPS):
        xcols = slice(g * GROUP_X, (g + 1) * GROUP_X)
        xg, bg, cg = early_convs[g] if g < EARLY_CONV_GROUPS else conv_group(g)
        cg_b = cg.astype(bf16)
        xg_b = xg.astype(bf16)
        s_old = state_scr[g]
        s_b = s_old.astype(bf16)

        cb = jax.lax.dot_general(cg_b, bg, (((1,), (1,)), ((), ())), preferred_element_type=f32)

        halves = []
        for half in range(HEADS_PER_GROUP // 2):
            hcols = slice(half * LANES, (half + 1) * LANES)
            rhs = jnp.concatenate([xg_b[:, hcols], s_b[:, hcols]], axis=0)
            outs = []
            for r in (2 * half, 2 * half + 1):
                h = g * HEADS_PER_GROUP + r
                col = jnp.broadcast_to(acs2[:, h:h + 1], (CHUNK, CHUNK))
                lm = jnp.exp2(col - rowarg_t[h:h + 1, :])
                m = (jnp.where(causal, lm, 0.0) * cb).astype(bf16)
                ce = (cg * jnp.exp2(col)).astype(bf16)
                outs.append(_dot(jnp.concatenate([m, ce], axis=1), rhs))
            halves.append(jnp.where(low_head, outs[0], outs[1]))
        y = jnp.concatenate(halves, axis=1)

        xw = (xg * w_exp[:, xcols]).astype(bf16)
        st = jax.lax.dot_general(bg, xw, (((0,), (0,)), ((), ())), preferred_element_type=f32)
        state_scr[g] = s_old * cd_exp[:, xcols] + st

        y = y + xg * dskip_ref[:, xcols]
        y = y * _silu_of_twice(z_ref[0, :, xcols].astype(f32))
        ms = jnp.mean(y * y, axis=-1, keepdims=True)
        y = y * jax.lax.rsqrt(ms + EPS) * nw_ref[:, xcols]
        o_ref[0, :, xcols] = y.astype(bf16)


def _ssd(layer, proj, dt_raw, cw, cb, dtb, alog, dskip, nw, eexp):
    bsz, lp, _ = proj.shape
    rb = CHUNK // CONV_HALO

    def cur(width, col):
        return pl.BlockSpec((1, CHUNK, width), lambda b, c: (b, c, col // width))

    def halo(width, col):
        return pl.BlockSpec((1, CONV_HALO, width),
                            lambda b, c: (b, jnp.maximum(c * rb - 1, 0), col // width))

    def const(shape):
        return pl.BlockSpec((None,) + shape, lambda b, c: (layer,) + (0,) * len(shape))

    n_bc = SSM_GROUPS * D_STATE
    return pl.pallas_call(
        _ssd_kernel,
        grid=(bsz, lp // CHUNK),
        in_specs=[
            cur(D_INNER, COL_XS), halo(D_INNER, COL_XS),
            cur(n_bc, COL_B), halo(n_bc, COL_B),
            cur(n_bc, COL_C), halo(n_bc, COL_C),
            cur(D_INNER, COL_Z),
            pl.BlockSpec((1, CHUNK, DT_LANES), lambda b, c: (b, c, 0)),
            const((CONV_WIDTH, D_INNER + 2 * n_bc)),
            const((1, D_INNER + 2 * n_bc)),
            const((1, DT_LANES)),
            const((SSM_HEADS, 1)),
            const((1, D_INNER)),
            const((1, D_INNER)),
            pl.BlockSpec((DT_LANES, D_INNER), lambda b, c: (0, 0)),
        ],
        out_specs=pl.BlockSpec((1, CHUNK, D_INNER), lambda b, c: (b, c, 0)),
        out_shape=jax.ShapeDtypeStruct((bsz, lp, D_INNER), bf16),
        scratch_shapes=[
            pltpu.VMEM((SSM_GROUPS, D_STATE, GROUP_X), f32),
        ],
        compiler_params=pltpu.CompilerParams(
            dimension_semantics=("parallel", "arbitrary"),
            vmem_limit_bytes=VMEM_LIMIT),
        name="ssd",
    )(proj, proj, proj, proj, proj, proj, proj, dt_raw, cw, cb, dtb, alog, dskip, nw, eexp)


def _merge_mlp_kernel(h_ref, gate_ref, ypool_ref, ynorm_ref, bgate_ref, wssd_ref, wo_ref,
                      mw_ref, ff1_ref, ff2_ref, fw_ref, o_ref, *, final):
    y_ssd = _dot(ynorm_ref[0], wssd_ref[...])
    gates = _sigmoid(gate_ref[0].astype(f32) + bgate_ref[...])
    mix = gates[:, :D_MODEL] * ypool_ref[0].astype(f32) + gates[:, D_MODEL:] * y_ssd
    h1 = h_ref[0] + _dot(mix.astype(bf16), wo_ref[...])
    ms = jnp.mean(h1 * h1, axis=-1, keepdims=True)
    v = (h1 * jax.lax.rsqrt(ms + EPS) * mw_ref[...]).astype(bf16)
    acc = h1
    for k in range(D_FF // FF_CHUNK):
        hid = jnp.maximum(_dot(v, ff1_ref[:, k * FF_CHUNK:(k + 1) * FF_CHUNK]), 0.0)
        acc = acc + _dot((hid * hid).astype(bf16), ff2_ref[k * FF_CHUNK:(k + 1) * FF_CHUNK, :])
    if final:
        ms = jnp.mean(acc * acc, axis=-1, keepdims=True)
        acc = acc * jax.lax.rsqrt(ms + EPS) * fw_ref[...]
    o_ref[0] = acc


def _merge_mlp(layer, h, proj, ypool, ynorm, bgate, wssd, wo, mw, ff1, ff2, fw):
    bsz, lp, _ = h.shape

    def const(shape):
        return pl.BlockSpec((None,) + shape, lambda b, i: (layer,) + (0,) * len(shape),
                            pipeline_mode=pl.Buffered(1))

    return pl.pallas_call(
        functools.partial(_merge_mlp_kernel, final=layer == DEPTH - 1),
        grid=(bsz, lp // TM_MLP),
        in_specs=[
            pl.BlockSpec((1, TM_MLP, D_MODEL), lambda b, i: (b, i, 0)),
            pl.BlockSpec((1, TM_MLP, 2 * D_MODEL), lambda b, i: (b, i, COL_GATE // (2 * D_MODEL))),
            pl.BlockSpec((1, TM_MLP, D_MODEL), lambda b, i: (b, i, 0)),
            pl.BlockSpec((1, TM_MLP, D_INNER), lambda b, i: (b, i, 0)),
            const((1, 2 * D_MODEL)),
            const((D_INNER, D_MODEL)),
            const((D_MODEL, D_MODEL)),
            const((1, D_MODEL)),
            const((D_MODEL, D_FF)),
            const((D_FF, D_MODEL)),
            pl.BlockSpec((1, D_MODEL), lambda b, i: (0, 0)),
        ],
        out_specs=pl.BlockSpec((1, TM_MLP, D_MODEL), lambda b, i: (b, i, 0)),
        out_shape=jax.ShapeDtypeStruct((bsz, lp, D_MODEL), f32),
        compiler_params=pltpu.CompilerParams(
            dimension_semantics=("parallel", "parallel"),
            vmem_limit_bytes=VMEM_LIMIT),
        name="merge_mlp",
    )(h, proj, ypool, ynorm, bgate, wssd, wo, mw, ff1, ff2, fw)


def _to_strided_rows(t):
    bsz, n, d = t.shape
    t = t.reshape(bsz, n // CHUNK, SUBLANES, REGS, d)
    return jnp.swapaxes(t, 2, 3).reshape(bsz, n, d)


def _from_strided_rows(t):
    bsz, n, d = t.shape
    t = t.reshape(bsz, n // CHUNK, REGS, SUBLANES, d)
    return jnp.swapaxes(t, 2, 3).reshape(bsz, n, d)


def kernel(x, meta_tokens, mix_norm_w, w_in, b_gate, pool_w_group, pool_scale, w_pool_up,
           conv_w, conv_b, dt_bias, a_log, d_skip, ssd_norm_w, w_ssd_out, w_o,
           mlp_norm_w, w_ff1, w_ff2, final_norm_w):
    bsz, seq, _ = x.shape
    meta = jnp.broadcast_to(meta_tokens[None].astype(x.dtype), (bsz, N_META, D_MODEL))
    first_chunk = jnp.concatenate([jnp.zeros((bsz, PAD_FRONT, D_MODEL), x.dtype), meta], axis=1)
    h = _to_strided_rows(jnp.concatenate([first_chunk, x], axis=1))

    w_t = jnp.swapaxes(w_in, 1, 2)
    row = lambda p: p[:, None, :]
    dtb = row(jnp.pad(dt_bias, ((0, 0), (0, DT_LANES - SSM_HEADS))))
    dskip = row(jnp.repeat(d_skip, SSM_HEAD_DIM, axis=-1))
    eexp = (jnp.arange(DT_LANES)[:, None] == (jnp.arange(D_INNER)[None, :] // SSM_HEAD_DIM)).astype(bf16)
    cw = conv_w * 0.5
    cb = row(conv_b * 0.5)
    alog = a_log[:, :, None]
    wg = pool_w_group.astype(bf16)
    wup = w_pool_up.astype(bf16)
    wssd = w_ssd_out.astype(bf16)
    wo = w_o.astype(bf16)
    ff1 = w_ff1.astype(bf16)
    ff2 = w_ff2.astype(bf16)
    mix_nw, pscale, ssd_nw = row(mix_norm_w), row(pool_scale), row(ssd_norm_w)
    bgate, mlp_nw = row(b_gate), row(mlp_norm_w)

    for i in range(DEPTH):
        proj, dt_raw = _in_proj(i, h, mix_nw, w_t)
        ypool = _pool(i, proj, wg, pscale, wup)
        ynorm = _ssd(i, proj, dt_raw, cw, cb, dtb, alog, dskip, ssd_nw, eexp)
        h = _merge_mlp(i, h, proj, ypool, ynorm, bgate, wssd, wo, mlp_nw, ff1, ff2, final_norm_w[None])
    return _from_strided_rows(h[:, h.shape[1] - seq:])
```

```python
import functools
import itertools
import math

import jax
import jax.numpy as jnp
from jax.experimental import pallas as pl
from jax.experimental.pallas import tpu as pltpu

D_MODEL = 1024
DEPTH = 4
N_META = 16
POOL_GROUPS = 4
POOL_GROUP_DIM = 256
POOL_WINDOWS = (2, 4, 8, 16)
D_INNER = 2048
SSM_HEAD_DIM = 64
SSM_HEADS = 32
SSM_GROUPS = 8
HEADS_PER_GROUP = 4
D_STATE = 128
CONV_WIDTH = 4
CHUNK = 128
D_FF = 4096
EPS = 1e-5
LOG2E = math.log2(math.e)

OFF_DT = 7168
OFF_GATE = 7200
W_BLOCK = 1024
WBLK_POOL, WBLK_Z, WBLK_XS, WBLK_B, WBLK_C = 0, 1, 3, 5, 6

COL_Z = 0
COL_XS = 2048
COL_GATE = 4096
COL_POOL = 6144
COL_B = 7168
COL_C = 8192
N_MAIN = 9216
DT_LANES = 128

PAD_FRONT = CHUNK - N_META
SUBLANES = 8
LANES = 128
REGS = CHUNK // SUBLANES
CONV_HALO = 32
GROUP_X = HEADS_PER_GROUP * SSM_HEAD_DIM

TM_PROJ = 1408
TM_POOL = 384
TM_MLP = 384
FF_CHUNK = 1024
VMEM_LIMIT = 56 * 1024 * 1024
MIXER_VMEM_LIMIT = 60 * 1024 * 1024

f32 = jnp.float32
bf16 = jnp.bfloat16


def _dot(a, b):
    return jnp.dot(a, b, preferred_element_type=f32)


def _token_of(q):
    return jnp.right_shift(q, 3) + jnp.bitwise_and(q, SUBLANES - 1) * REGS


def _split3(x):
    p1 = x.astype(bf16)
    r1 = x - p1.astype(f32)
    p2 = r1.astype(bf16)
    p3 = (r1 - p2.astype(f32)).astype(bf16)
    return p1, p2, p3


def _dot_exact_rhs(x, e):
    p1, p2, p3 = _split3(x)
    return _dot(p1, e) + _dot(p2, e) + _dot(p3, e)


def _sigmoid(x):
    return 1.0 / (1.0 + jnp.exp(-x))


def _silu_of_twice(hx):
    return hx + hx * jnp.tanh(hx)


def _softplus(x):
    return jnp.maximum(x, 0.0) + jnp.log1p(jnp.exp(-jnp.abs(x)))


GATE_BLK0 = COL_GATE // W_BLOCK
NT_DIMS = (((1,), (1,)), ((), ()))


def _in_proj_kernel(h_ref, nw_ref, w_ref, wdt_ref, o_ref, dt_ref, u_scr):
    j = pl.program_id(1)
    i = pl.program_id(2)
    rows = pl.ds(pl.multiple_of(i * TM_PROJ, CHUNK), TM_PROJ)

    @pl.when(j == 0)
    def _():
        x = h_ref[0]
        ms = jnp.mean(x * x, axis=-1, keepdims=True)
        u = x * jax.lax.rsqrt(ms + EPS) * nw_ref[...]
        r = i * TM_PROJ + jax.lax.broadcasted_iota(jnp.int32, (TM_PROJ, 1), 0)
        valid = jnp.logical_or(r >= CHUNK, jnp.bitwise_and(r, SUBLANES - 1) == SUBLANES - 1)
        u = jnp.where(valid, u, 0.0).astype(bf16)
        u_scr[rows, :] = u
        dt_ref[0] = jax.lax.dot_general(u, wdt_ref[0].astype(bf16), NT_DIMS, preferred_element_type=f32)

    w = w_ref[0].astype(bf16)
    acc = jax.lax.dot_general(u_scr[rows, :], w, NT_DIMS, preferred_element_type=f32)
    o_ref[0] = (acc * jnp.where(j < COL_XS // W_BLOCK, 0.5, 1.0)).astype(bf16)


def _w_row_of(j):
    blk = jnp.where(j < GATE_BLK0, j + WBLK_Z, jnp.where(j == COL_POOL // W_BLOCK, WBLK_POOL, j - 2))
    is_gate = jnp.logical_and(j >= GATE_BLK0, j < GATE_BLK0 + 2)
    return pl.multiple_of(jnp.where(is_gate, OFF_GATE + (j - GATE_BLK0) * W_BLOCK, blk * W_BLOCK), SUBLANES)


def _in_proj(layer, h, nw, w_t):
    bsz, lp, _ = h.shape
    n_i = lp // TM_PROJ
    grid = (bsz, N_MAIN // W_BLOCK, n_i)

    def first_pass_tile(b, j, i):
        return (b, jnp.where(j == 0, i, n_i - 1), 0)

    elem = lambda *shape: tuple(pl.Element(s) for s in shape)
    return pl.pallas_call(
        _in_proj_kernel,
        grid=grid,
        in_specs=[
            pl.BlockSpec((1, TM_PROJ, D_MODEL), first_pass_tile),
            pl.BlockSpec((None, 1, D_MODEL), lambda b, j, i: (layer, 0, 0)),
            pl.BlockSpec(elem(1, W_BLOCK, D_MODEL), lambda b, j, i: (layer, _w_row_of(j), 0)),
            pl.BlockSpec(elem(1, DT_LANES, D_MODEL), lambda b, j, i: (layer, OFF_DT, 0)),
        ],
        out_specs=[
            pl.BlockSpec((1, TM_PROJ, W_BLOCK), lambda b, j, i: (b, i, j)),
            pl.BlockSpec((1, TM_PROJ, DT_LANES), first_pass_tile),
        ],
        out_shape=[
            jax.ShapeDtypeStruct((bsz, lp, N_MAIN), bf16),
            jax.ShapeDtypeStruct((bsz, lp, DT_LANES), f32),
        ],
        scratch_shapes=[pltpu.VMEM((lp, D_MODEL), bf16)],
        compiler_params=pltpu.CompilerParams(
            dimension_semantics=("parallel", "arbitrary", "arbitrary"),
            vmem_limit_bytes=VMEM_LIMIT),
        name="in_proj",
    )(h, nw, w_t, w_t)


def _pool_kernel(u_ref, prev_ref, wg_ref, scale_ref, wup_ref, o_ref, pooled_scr, y1_scr):
    i = pl.program_id(1)
    prev_chunk = prev_ref[0]
    prev_chunk = jnp.where(i > 0, prev_chunk, jnp.zeros_like(prev_chunk))

    q_io = jax.lax.broadcasted_iota(jnp.int32, (CHUNK, 2 * CHUNK), 0)
    k_io = jax.lax.broadcasted_iota(jnp.int32, (CHUNK, 2 * CHUNK), 1)
    k_tok = _token_of(jnp.bitwise_and(k_io, CHUNK - 1)) + jnp.right_shift(k_io, 7) * CHUNK
    delta = _token_of(q_io) + CHUNK - k_tok
    tok_in_chunk = _token_of(jax.lax.broadcasted_iota(jnp.int32, (CHUNK, 1), 0))
    for g, win in enumerate(POOL_WINDOWS):
        band = jnp.where(jnp.logical_and(delta >= 0, delta < win), 1.0, 0.0).astype(bf16)
        cols = slice(g * POOL_GROUP_DIM, (g + 1) * POOL_GROUP_DIM)
        for sb in range(TM_POOL // CHUNK):
            rows = slice(sb * CHUNK, (sb + 1) * CHUNK)
            before = prev_chunk[:, cols] if sb == 0 else u_ref[0, (sb - 1) * CHUNK:sb * CHUNK, cols]
            cur = u_ref[0, rows, cols]
            wsum = _dot(band, jnp.concatenate([before, cur], axis=0))
            tok = i * TM_POOL + sb * CHUNK + tok_in_chunk - PAD_FRONT
            cnt = jnp.clip(tok + 1, 1, win).astype(f32)
            pooled_scr[rows, cols] = (wsum / cnt - cur.astype(f32)).astype(bf16)
        y1 = _dot(pooled_scr[:, cols], wg_ref[g]) * scale_ref[:, cols]
        y1_scr[:, cols] = y1.astype(bf16)
    o_ref[0] = _dot(y1_scr[...], wup_ref[...]).astype(bf16)


def _pool(layer, proj, wg, scale, wup):
    bsz, lp, _ = proj.shape
    rb = TM_POOL // CHUNK
    return pl.pallas_call(
        _pool_kernel,
        grid=(bsz, lp // TM_POOL),
        in_specs=[
            pl.BlockSpec((1, TM_POOL, D_MODEL), lambda b, i: (b, i, COL_POOL // D_MODEL)),
            pl.BlockSpec((1, CHUNK, D_MODEL),
                         lambda b, i: (b, jnp.maximum(i * rb - 1, 0), COL_POOL // D_MODEL)),
            pl.BlockSpec((None, POOL_GROUPS, POOL_GROUP_DIM, POOL_GROUP_DIM),
                         lambda b, i: (layer, 0, 0, 0)),
            pl.BlockSpec((None, 1, D_MODEL), lambda b, i: (layer, 0, 0)),
            pl.BlockSpec((None, D_MODEL, D_MODEL), lambda b, i: (layer, 0, 0)),
        ],
        out_specs=pl.BlockSpec((1, TM_POOL, D_MODEL), lambda b, i: (b, i, 0)),
        out_shape=jax.ShapeDtypeStruct((bsz, lp, D_MODEL), bf16),
        scratch_shapes=[
            pltpu.VMEM((TM_POOL, D_MODEL), bf16),
            pltpu.VMEM((TM_POOL, D_MODEL), bf16),
        ],
        compiler_params=pltpu.CompilerParams(
            dimension_semantics=("parallel", "parallel"),
            vmem_limit_bytes=VMEM_LIMIT),
        name="pool",
    )(proj, proj, wg, scale, wup)


def _conv_silu(halo, cur, w, b, use_halo):
    n = cur.shape[1]
    tail = CONV_WIDTH - 1
    x3 = cur.astype(f32).reshape(REGS, SUBLANES, n)
    prev_tail = halo[CONV_HALO - tail * SUBLANES:, :].astype(f32)
    prev_tail = jnp.where(use_halo, prev_tail, 0.0).reshape(tail, SUBLANES, n)
    merged = jnp.concatenate([x3[REGS - tail:, :SUBLANES - 1, :], prev_tail[:, SUBLANES - 1:, :]], axis=1)
    wrapped = pltpu.roll(merged, 1, axis=1)
    acc = b.reshape(1, 1, n) + x3 * w[CONV_WIDTH - 1:CONV_WIDTH, :].reshape(1, 1, n)
    for d in range(1, CONV_WIDTH):
        shifted = jnp.concatenate([wrapped[tail - d:], x3[:REGS - d]], axis=0)
        acc = acc + shifted * w[CONV_WIDTH - 1 - d:CONV_WIDTH - d, :].reshape(1, 1, n)
    return _silu_of_twice(acc).reshape(CHUNK, n)


def _chunk_has_predecessor(k, seq_start):
    return jnp.logical_not(jnp.logical_and(seq_start, k == 0))


def _ssd_group_front(k, g, seq_start, ssd_refs):
    (xs_ref, xs_h_ref, b_ref, b_h_ref, c_ref, c_h_ref, _, _, cw_ref, cb_ref) = ssd_refs[:10]
    rows = slice(k * CHUNK, (k + 1) * CHUNK)
    use_halo = _chunk_has_predecessor(k, seq_start)

    def conv_of(cur_ref, halo_ref, cols, wcols):
        if k == 0:
            halo = halo_ref[0, :, cols]
        else:
            halo = cur_ref[0, k * CHUNK - CONV_HALO:k * CHUNK, cols]
        return _conv_silu(halo, cur_ref[0, rows, cols], cw_ref[:, wcols], cb_ref[:, wcols], use_halo)

    xcols = slice(g * GROUP_X, (g + 1) * GROUP_X)
    ncols = slice(g * D_STATE, (g + 1) * D_STATE)
    bc_b = slice(D_INNER + g * D_STATE, D_INNER + (g + 1) * D_STATE)
    bc_c = slice(D_INNER + SSM_GROUPS * D_STATE + g * D_STATE,
                 D_INNER + SSM_GROUPS * D_STATE + (g + 1) * D_STATE)
    xg = conv_of(xs_ref, xs_h_ref, xcols, xcols)
    bg = conv_of(b_ref, b_h_ref, ncols, bc_b).astype(bf16)
    cg = conv_of(c_ref, c_h_ref, ncols, bc_c)
    cb = jax.lax.dot_general(cg.astype(bf16), bg, NT_DIMS, preferred_element_type=f32)
    return xg, bg, cg, cb


def _ssd_prologue(k, seq_start, ssd_refs):
    dt_ref, dtb_ref, alog_ref, eexp_ref = ssd_refs[7], ssd_refs[10], ssd_refs[11], ssd_refs[14]
    rows = slice(k * CHUNK, (k + 1) * CHUNK)
    use_halo = _chunk_has_predecessor(k, seq_start)

    tok = _token_of(jax.lax.broadcasted_iota(jnp.int32, (1, CHUNK), 1))
    valid_t = jnp.logical_or(use_halo, tok >= PAD_FRONT)
    raw_t = (dt_ref[0, rows, :] + dtb_ref[...]).T[0:SSM_HEADS, :]
    dt_t = jnp.where(valid_t, _softplus(raw_t), 0.0)
    a_col = -jnp.exp(alog_ref[...])
    tok_r = _token_of(jax.lax.broadcasted_iota(jnp.int32, (CHUNK, CHUNK), 0))
    tok_c = _token_of(jax.lax.broadcasted_iota(jnp.int32, (CHUNK, CHUNK), 1))
    causal = tok_r >= tok_c
    tri_t = jnp.where(tok_r <= tok_c, 1.0, 0.0).astype(bf16)
    acs_t = _dot_exact_rhs(dt_t * a_col, tri_t)
    acs_last = acs_t[:, CHUNK - 1:CHUNK]
    rowarg_t = acs_t * LOG2E - jnp.log(dt_t) * LOG2E
    w_state_t = jnp.exp(acs_last - acs_t) * dt_t
    e_acs_last = jnp.exp(acs_last)

    def tokens_on_rows(x):
        pad = jnp.zeros((CHUNK - SSM_HEADS, CHUNK), f32)
        return jnp.concatenate([x, pad], axis=0).T

    acs2 = tokens_on_rows(acs_t * LOG2E)
    w_exp = _dot(tokens_on_rows(w_state_t).astype(bf16), eexp_ref[...])
    chunk_decay = tokens_on_rows(jnp.broadcast_to(e_acs_last, (SSM_HEADS, CHUNK)))[0:SUBLANES, :]
    cd_exp = _dot_exact_rhs(chunk_decay, eexp_ref[...])[0:1, :]

    return acs2, rowarg_t, w_exp, cd_exp, causal


def _ssd_group_lhs(g, front, prologue):
    _, _, cg, cb = front
    acs2, rowarg_t, _, _, causal = prologue
    lhs = []
    for r in range(HEADS_PER_GROUP):
        h = g * HEADS_PER_GROUP + r
        col = jnp.broadcast_to(acs2[:, h:h + 1], (CHUNK, CHUNK))
        lm = jnp.exp2(col - rowarg_t[h:h + 1, :])
        m = (jnp.where(causal, lm, 0.0) * cb).astype(bf16)
        ce = (cg * jnp.exp2(col)).astype(bf16)
        lhs.append(jnp.concatenate([m, ce], axis=1))
    return lhs


def _ssd_group_back(k, g, front, lhs, prologue, ssd_refs, state_scr, y_ref):
    z_ref, dskip_ref, nw_ref = ssd_refs[6], ssd_refs[12], ssd_refs[13]
    xg, bg, _, _ = front
    _, _, w_exp, cd_exp, _ = prologue
    rows = slice(k * CHUNK, (k + 1) * CHUNK)
    xcols = slice(g * GROUP_X, (g + 1) * GROUP_X)
    low_head = jax.lax.broadcasted_iota(jnp.int32, (1, LANES), 1) < SSM_HEAD_DIM
    xg_b = xg.astype(bf16)
    s_old = state_scr[g]
    s_b = s_old.astype(bf16)

    halves = []
    for half in range(HEADS_PER_GROUP // 2):
        hcols = slice(half * LANES, (half + 1) * LANES)
        rhs = jnp.concatenate([xg_b[:, hcols], s_b[:, hcols]], axis=0)
        outs = [_dot(lhs[r], rhs) for r in (2 * half, 2 * half + 1)]
        halves.append(jnp.where(low_head, outs[0], outs[1]))
    y = jnp.concatenate(halves, axis=1)

    xw = (xg * w_exp[:, xcols]).astype(bf16)
    st = jax.lax.dot_general(bg, xw, (((0,), (0,)), ((), ())), preferred_element_type=f32)
    state_scr[g] = s_old * cd_exp[:, xcols] + st

    y = y + xg * dskip_ref[:, xcols]
    y = y * _silu_of_twice(z_ref[0, rows, xcols].astype(f32))
    ms = jnp.mean(y * y, axis=-1, keepdims=True)
    y = y * jax.lax.rsqrt(ms + EPS) * nw_ref[:, xcols]
    y_ref[rows, xcols] = y.astype(bf16)


def _ssd_tile(seq_start, ssd_refs, state_scr, y_ref):
    n_chunks = TM_MLP // CHUNK
    items = [(k, g) for k in range(n_chunks) for g in range(SSM_GROUPS)]
    prologues = {0: _ssd_prologue(0, seq_start, ssd_refs)}
    fronts = {items[0]: _ssd_group_front(*items[0], seq_start, ssd_refs)}
    yield
    for n, (k, g) in enumerate(items):
        if n + 1 < len(items):
            fronts[items[n + 1]] = _ssd_group_front(*items[n + 1], seq_start, ssd_refs)
        if g == SSM_GROUPS // 2 and k + 1 < n_chunks:
            prologues[k + 1] = _ssd_prologue(k + 1, seq_start, ssd_refs)
        lhs = _ssd_group_lhs(g, fronts[(k, g)], prologues[k])
        _ssd_group_back(k, g, fronts.pop((k, g)), lhs, prologues[k], ssd_refs, state_scr, y_ref)
        yield


def _merge_mlp_body(h_ref, gate_ref, ypool_ref, ynorm_ref, bgate_ref, wssd_ref, wo_ref,
                    mw_ref, ff1_ref, ff2_ref, fw_ref, o_ref, final):
    def dot_in_pieces(a, w_ref, rows, n_pieces):
        width = w_ref.shape[1] // n_pieces
        out = []
        for p in range(n_pieces):
            out.append(_dot(a, w_ref[rows, p * width:(p + 1) * width]))
            yield None
        yield jnp.concatenate(out, axis=1)

    every = slice(None)
    ynorm = ynorm_ref[...]
    pieces = dot_in_pieces(ynorm, wssd_ref, every, 4)
    for _ in range(4):
        yield next(pieces)
    y_ssd = next(pieces)
    gates = _sigmoid(gate_ref[0].astype(f32) + bgate_ref[...])
    mix = (gates[:, :D_MODEL] * ypool_ref[0].astype(f32) + gates[:, D_MODEL:] * y_ssd).astype(bf16)
    pieces = dot_in_pieces(mix, wo_ref, every, 2)
    for _ in range(2):
        yield next(pieces)
    h1 = h_ref[0] + next(pieces)
    ms = jnp.mean(h1 * h1, axis=-1, keepdims=True)
    v = (h1 * jax.lax.rsqrt(ms + EPS) * mw_ref[...]).astype(bf16)
    acc = h1
    for k in range(D_FF // FF_CHUNK):
        ff = slice(k * FF_CHUNK, (k + 1) * FF_CHUNK)
        half = FF_CHUNK // 2
        hid = []
        for p in range(2):
            hid.append(jnp.maximum(_dot(v, ff1_ref[:, k * FF_CHUNK + p * half:k * FF_CHUNK + (p + 1) * half]), 0.0))
            yield
        hid = jnp.concatenate(hid, axis=1)
        hid = (hid * hid).astype(bf16)
        pieces = dot_in_pieces(hid, ff2_ref, ff, 2)
        for _ in range(2):
            yield next(pieces)
        acc = acc + next(pieces)
    if final:
        ms = jnp.mean(acc * acc, axis=-1, keepdims=True)
        acc = acc * jax.lax.rsqrt(ms + EPS) * fw_ref[...]
    o_ref[0] = acc


N_SSD_REFS = 15
SCAN_STAGES_PER_MERGE_STAGE = 1
_DONE = object()


def _mixer_mlp_kernel(*refs, final):
    ssd_refs = refs[:N_SSD_REFS]
    (h_ref, gate_ref, ypool_ref, bgate_ref, wssd_ref, wo_ref, mw_ref, ff1_ref, ff2_ref, fw_ref,
     o_ref, state_scr, ybuf) = refs[N_SSD_REFS:]
    s = pl.program_id(1)

    @pl.when(s == 0)
    def _():
        state_scr[...] = jnp.zeros_like(state_scr)
        ybuf[...] = jnp.zeros_like(ybuf)

    write_slot = jnp.bitwise_and(s, 1)
    merge = _merge_mlp_body(h_ref, gate_ref, ypool_ref, ybuf.at[1 - write_slot], bgate_ref, wssd_ref,
                            wo_ref, mw_ref, ff1_ref, ff2_ref, fw_ref, o_ref, final)
    scan = _ssd_tile(s == 0, ssd_refs, state_scr, ybuf.at[write_slot])
    pending = [scan, merge]
    while pending:
        for gen, stages in ((scan, SCAN_STAGES_PER_MERGE_STAGE), (merge, 1)):
            for _ in range(stages):
                if gen in pending and next(gen, _DONE) is _DONE:
                    pending.remove(gen)


def _mixer_mlp(layer, h, proj, dt_raw, ypool, cw, cb, dtb, alog, dskip, ssd_nw, eexp,
               bgate, wssd, wo, mw, ff1, ff2, fw):
    bsz, lp, _ = h.shape
    n_tiles = lp // TM_MLP
    hb = TM_MLP // CONV_HALO

    def scan_tile(b, s):
        return jnp.minimum(s, n_tiles - 1)

    def merge_tile(b, s):
        return jnp.maximum(s - 1, 0)

    def cur(width, col):
        return pl.BlockSpec((1, TM_MLP, width), lambda b, s: (b, scan_tile(b, s), col // width))

    def halo(width, col):
        return pl.BlockSpec((1, CONV_HALO, width),
                            lambda b, s: (b, jnp.maximum(scan_tile(b, s) * hb - 1, 0), col // width))

    def const(shape):
        return pl.BlockSpec((None,) + shape, lambda b, s: (layer,) + (0,) * len(shape),
                            pipeline_mode=pl.Buffered(1))

    def merge_in(width, col):
        return pl.BlockSpec((1, TM_MLP, width), lambda b, s: (b, merge_tile(b, s), col // width))

    n_bc = SSM_GROUPS * D_STATE
    return pl.pallas_call(
        functools.partial(_mixer_mlp_kernel, final=layer == DEPTH - 1),
        grid=(bsz, n_tiles + 1),
        in_specs=[
            cur(D_INNER, COL_XS), halo(D_INNER, COL_XS),
            cur(n_bc, COL_B), halo(n_bc, COL_B),
            cur(n_bc, COL_C), halo(n_bc, COL_C),
            cur(D_INNER, COL_Z),
            pl.BlockSpec((1, TM_MLP, DT_LANES), lambda b, s: (b, scan_tile(b, s), 0)),
            const((CONV_WIDTH, D_INNER + 2 * n_bc)),
            const((1, D_INNER + 2 * n_bc)),
            const((1, DT_LANES)),
            const((SSM_HEADS, 1)),
            const((1, D_INNER)),
            const((1, D_INNER)),
            pl.BlockSpec((DT_LANES, D_INNER), lambda b, s: (0, 0), pipeline_mode=pl.Buffered(1)),
            merge_in(D_MODEL, 0),
            merge_in(2 * D_MODEL, COL_GATE),
            merge_in(D_MODEL, 0),
            const((1, 2 * D_MODEL)),
            const((D_INNER, D_MODEL)),
            const((D_MODEL, D_MODEL)),
            const((1, D_MODEL)),
            const((D_MODEL, D_FF)),
            const((D_FF, D_MODEL)),
            pl.BlockSpec((1, D_MODEL), lambda b, s: (0, 0), pipeline_mode=pl.Buffered(1)),
        ],
        out_specs=pl.BlockSpec((1, TM_MLP, D_MODEL), lambda b, s: (b, merge_tile(b, s), 0)),
        out_shape=jax.ShapeDtypeStruct((bsz, lp, D_MODEL), f32),
        scratch_shapes=[
            pltpu.VMEM((SSM_GROUPS, D_STATE, GROUP_X), f32),
            pltpu.VMEM((2, TM_MLP, D_INNER), bf16),
        ],
        compiler_params=pltpu.CompilerParams(
            dimension_semantics=("parallel", "arbitrary"),
            vmem_limit_bytes=MIXER_VMEM_LIMIT),
        name="mixer_mlp",
    )(proj, proj, proj, proj, proj, proj, proj, dt_raw, cw, cb, dtb, alog, dskip, ssd_nw, eexp,
      h, proj, ypool, bgate, wssd, wo, mw, ff1, ff2, fw)


def _to_strided_rows(t):
    bsz, n, d = t.shape
    t = t.reshape(bsz, n // CHUNK, SUBLANES, REGS, d)
    return jnp.swapaxes(t, 2, 3).reshape(bsz, n, d)


def _from_strided_rows(t):
    bsz, n, d = t.shape
    t = t.reshape(bsz, n // CHUNK, REGS, SUBLANES, d)
    return jnp.swapaxes(t, 2, 3).reshape(bsz, n, d)


def kernel(x, meta_tokens, mix_norm_w, w_in, b_gate, pool_w_group, pool_scale, w_pool_up,
           conv_w, conv_b, dt_bias, a_log, d_skip, ssd_norm_w, w_ssd_out, w_o,
           mlp_norm_w, w_ff1, w_ff2, final_norm_w):
    bsz, seq, _ = x.shape
    meta = jnp.broadcast_to(meta_tokens[None].astype(x.dtype), (bsz, N_META, D_MODEL))
    first_chunk = jnp.concatenate([jnp.zeros((bsz, PAD_FRONT, D_MODEL), x.dtype), meta], axis=1)
    h = _to_strided_rows(jnp.concatenate([first_chunk, x], axis=1))

    w_t = jnp.swapaxes(w_in, 1, 2)
    row = lambda p: p[:, None, :]
    dtb = row(jnp.pad(dt_bias, ((0, 0), (0, DT_LANES - SSM_HEADS))))
    dskip = row(jnp.repeat(d_skip, SSM_HEAD_DIM, axis=-1))
    eexp = (jnp.arange(DT_LANES)[:, None] == (jnp.arange(D_INNER)[None, :] // SSM_HEAD_DIM)).astype(bf16)
    cw = conv_w * 0.5
    cb = row(conv_b * 0.5)
    alog = a_log[:, :, None]
    wg = pool_w_group.astype(bf16)
    wup = w_pool_up.astype(bf16)
    wssd = w_ssd_out.astype(bf16)
    wo = w_o.astype(bf16)
    ff1 = w_ff1.astype(bf16)
    ff2 = w_ff2.astype(bf16)
    mix_nw, pscale, ssd_nw = row(mix_norm_w), row(pool_scale), row(ssd_norm_w)
    bgate, mlp_nw = row(b_gate), row(mlp_norm_w)

    for i in range(DEPTH):
        proj, dt_raw = _in_proj(i, h, mix_nw, w_t)
        ypool = _pool(i, proj, wg, pscale, wup)
        h = _mixer_mlp(i, h, proj, dt_raw, ypool, cw, cb, dtb, alog, dskip, ssd_nw, eexp,
                       bgate, wssd, wo, mlp_nw, ff1, ff2, final_norm_w[None])
    return _from_strided_rows(h[:, h.shape[1] - seq:])
```

```python
import functools
import itertools
import math

import jax
import jax.numpy as jnp
from jax.experimental import pallas as pl
from jax.experimental.pallas import tpu as pltpu

D_MODEL = 1024
DEPTH = 4
N_META = 16
POOL_GROUPS = 4
POOL_GROUP_DIM = 256
POOL_WINDOWS = (2, 4, 8, 16)
D_INNER = 2048
SSM_HEAD_DIM = 64
SSM_HEADS = 32
SSM_GROUPS = 8
HEADS_PER_GROUP = 4
D_STATE = 128
CONV_WIDTH = 4
CHUNK = 128
D_FF = 4096
EPS = 1e-5
LOG2E = math.log2(math.e)

OFF_DT = 7168
OFF_GATE = 7200
W_BLOCK = 1024
WBLK_POOL, WBLK_Z, WBLK_XS, WBLK_B, WBLK_C = 0, 1, 3, 5, 6

COL_Z = 0
COL_XS = 2048
COL_GATE = 4096
COL_POOL = 6144
COL_B = 7168
COL_C = 8192
N_MAIN = 9216
DT_LANES = 128

PAD_FRONT = CHUNK - N_META
SUBLANES = 8
LANES = 128
REGS = CHUNK // SUBLANES
CONV_HALO = 32
GROUP_X = HEADS_PER_GROUP * SSM_HEAD_DIM

TM_PROJ = 1408
TM_POOL = 384
TM_MLP = 384
FF_CHUNK = 1024
VMEM_LIMIT = 56 * 1024 * 1024
MIXER_VMEM_LIMIT = 60 * 1024 * 1024

f32 = jnp.float32
bf16 = jnp.bfloat16


def _dot(a, b):
    return jnp.dot(a, b, preferred_element_type=f32)


def _token_of(q):
    return jnp.right_shift(q, 3) + jnp.bitwise_and(q, SUBLANES - 1) * REGS


def _split3(x):
    p1 = x.astype(bf16)
    r1 = x - p1.astype(f32)
    p2 = r1.astype(bf16)
    p3 = (r1 - p2.astype(f32)).astype(bf16)
    return p1, p2, p3


def _dot_exact_rhs(x, e):
    p1, p2, p3 = _split3(x)
    return _dot(p1, e) + _dot(p2, e) + _dot(p3, e)


def _sigmoid(x):
    return 1.0 / (1.0 + jnp.exp(-x))


def _silu_of_twice(hx):
    return hx + hx * jnp.tanh(hx)


def _softplus(x):
    return jnp.maximum(x, 0.0) + jnp.log1p(jnp.exp(-jnp.abs(x)))


GATE_BLK0 = COL_GATE // W_BLOCK
NT_DIMS = (((1,), (1,)), ((), ()))


CAST_PLAN = (
    ((32, D_FF), D_MODEL // 32),
    ((128, D_MODEL), D_FF // 128),
    ((128, D_MODEL), D_INNER // 128),
    ((128, D_MODEL), D_MODEL // 128),
    ((128, D_MODEL), D_MODEL // 128),
    ((POOL_GROUPS, POOL_GROUP_DIM, POOL_GROUP_DIM), 1),
)
CAST_STARTS = tuple(sum(n for _, n in CAST_PLAN[:p]) for p in range(len(CAST_PLAN)))
N_CAST = len(CAST_PLAN)


def _in_proj_kernel(h_ref, nw_ref, w_ref, wdt_ref, *rest):
    cast_in = rest[:N_CAST]
    o_ref, dt_ref = rest[N_CAST:N_CAST + 2]
    cast_out = rest[N_CAST + 2:2 * N_CAST + 2]
    u_scr = rest[-1]
    j = pl.program_id(1)
    i = pl.program_id(2)
    rows = pl.ds(pl.multiple_of(i * TM_PROJ, CHUNK), TM_PROJ)

    step = (pl.program_id(0) * pl.num_programs(1) + j) * pl.num_programs(2) + i
    for src, dst, start, (_, n_blocks) in zip(cast_in, cast_out, CAST_STARTS, CAST_PLAN):
        @pl.when(jnp.logical_and(step >= start, step < start + n_blocks))
        def _(src=src, dst=dst):
            dst[...] = src[...].astype(bf16)

    @pl.when(j == 0)
    def _():
        x = h_ref[0]
        ms = jnp.mean(x * x, axis=-1, keepdims=True)
        u = x * jax.lax.rsqrt(ms + EPS) * nw_ref[...]
        r = i * TM_PROJ + jax.lax.broadcasted_iota(jnp.int32, (TM_PROJ, 1), 0)
        valid = jnp.logical_or(r >= CHUNK, jnp.bitwise_and(r, SUBLANES - 1) == SUBLANES - 1)
        u = jnp.where(valid, u, 0.0).astype(bf16)
        u_scr[rows, :] = u
        dt_ref[0] = jax.lax.dot_general(u, wdt_ref[0].astype(bf16), NT_DIMS, preferred_element_type=f32)

    w = w_ref[0].astype(bf16)
    acc = jax.lax.dot_general(u_scr[rows, :], w, NT_DIMS, preferred_element_type=f32)
    o_ref[0] = (acc * jnp.where(j < COL_XS // W_BLOCK, 0.5, 1.0)).astype(bf16)


def _w_row_of(j):
    blk = jnp.where(j < GATE_BLK0, j + WBLK_Z, jnp.where(j == COL_POOL // W_BLOCK, WBLK_POOL, j - 2))
    is_gate = jnp.logical_and(j >= GATE_BLK0, j < GATE_BLK0 + 2)
    return pl.multiple_of(jnp.where(is_gate, OFF_GATE + (j - GATE_BLK0) * W_BLOCK, blk * W_BLOCK), SUBLANES)


def _in_proj(layer, h, nw, w_t, cast_weights):
    bsz, lp, _ = h.shape
    n_i = lp // TM_PROJ
    n_j = N_MAIN // W_BLOCK
    grid = (bsz, n_j, n_i)
    assert CAST_STARTS[-1] + CAST_PLAN[-1][1] <= bsz * n_j * n_i

    def first_pass_tile(b, j, i):
        return (b, jnp.where(j == 0, i, n_i - 1), 0)

    def cast_block(start, n_blocks):
        return lambda b, j, i: jnp.clip((b * n_j + j) * n_i + i - start, 0, n_blocks - 1)

    cast_in, cast_out, cast_shapes = [], [], []
    for w, start, (blk, n_blocks) in zip(cast_weights, CAST_STARTS, CAST_PLAN):
        which = cast_block(start, n_blocks)
        tail = (0,) * (len(blk) - 1)
        cast_in.append(pl.BlockSpec((None,) + blk, lambda b, j, i, which=which, tail=tail:
                                    (layer, which(b, j, i)) + tail))
        cast_out.append(pl.BlockSpec(blk, lambda b, j, i, which=which, tail=tail: (which(b, j, i),) + tail))
        cast_shapes.append(jax.ShapeDtypeStruct(w.shape[1:], bf16))

    elem = lambda *shape: tuple(pl.Element(s) for s in shape)
    proj, dt_raw, *casts = pl.pallas_call(
        _in_proj_kernel,
        grid=grid,
        in_specs=[
            pl.BlockSpec((1, TM_PROJ, D_MODEL), first_pass_tile),
            pl.BlockSpec((None, 1, D_MODEL), lambda b, j, i: (layer, 0, 0)),
            pl.BlockSpec(elem(1, W_BLOCK, D_MODEL), lambda b, j, i: (layer, _w_row_of(j), 0)),
            pl.BlockSpec(elem(1, DT_LANES, D_MODEL), lambda b, j, i: (layer, OFF_DT, 0)),
        ] + cast_in,
        out_specs=[
            pl.BlockSpec((1, TM_PROJ, W_BLOCK), lambda b, j, i: (b, i, j)),
            pl.BlockSpec((1, TM_PROJ, DT_LANES), first_pass_tile),
        ] + cast_out,
        out_shape=[
            jax.ShapeDtypeStruct((bsz, lp, N_MAIN), bf16),
            jax.ShapeDtypeStruct((bsz, lp, DT_LANES), f32),
        ] + cast_shapes,
        scratch_shapes=[pltpu.VMEM((lp, D_MODEL), bf16)],
        compiler_params=pltpu.CompilerParams(
            dimension_semantics=("arbitrary", "arbitrary", "arbitrary"),
            vmem_limit_bytes=VMEM_LIMIT),
        name="in_proj",
    )(h, nw, w_t, w_t, *cast_weights)
    return proj, dt_raw, casts


def _pool_kernel(u_ref, prev_ref, wg_ref, scale_ref, wup_ref, o_ref, pooled_scr, y1_scr):
    i = pl.program_id(1)
    prev_chunk = prev_ref[0]
    prev_chunk = jnp.where(i > 0, prev_chunk, jnp.zeros_like(prev_chunk))

    q_io = jax.lax.broadcasted_iota(jnp.int32, (CHUNK, 2 * CHUNK), 0)
    k_io = jax.lax.broadcasted_iota(jnp.int32, (CHUNK, 2 * CHUNK), 1)
    k_tok = _token_of(jnp.bitwise_and(k_io, CHUNK - 1)) + jnp.right_shift(k_io, 7) * CHUNK
    delta = _token_of(q_io) + CHUNK - k_tok
    tok_in_chunk = _token_of(jax.lax.broadcasted_iota(jnp.int32, (CHUNK, 1), 0))
    for g, win in enumerate(POOL_WINDOWS):
        band = jnp.where(jnp.logical_and(delta >= 0, delta < win), 1.0, 0.0).astype(bf16)
        cols = slice(g * POOL_GROUP_DIM, (g + 1) * POOL_GROUP_DIM)
        for sb in range(TM_POOL // CHUNK):
            rows = slice(sb * CHUNK, (sb + 1) * CHUNK)
            before = prev_chunk[:, cols] if sb == 0 else u_ref[0, (sb - 1) * CHUNK:sb * CHUNK, cols]
            cur = u_ref[0, rows, cols]
            wsum = _dot(band, jnp.concatenate([before, cur], axis=0))
            tok = i * TM_POOL + sb * CHUNK + tok_in_chunk - PAD_FRONT
            cnt = jnp.clip(tok + 1, 1, win).astype(f32)
            pooled_scr[rows, cols] = (wsum / cnt - cur.astype(f32)).astype(bf16)
        y1 = _dot(pooled_scr[:, cols], wg_ref[g]) * scale_ref[:, cols]
        y1_scr[:, cols] = y1.astype(bf16)
    o_ref[0] = _dot(y1_scr[...], wup_ref[...]).astype(bf16)


def _pool(layer, proj, wg, scale, wup):
    bsz, lp, _ = proj.shape
    rb = TM_POOL // CHUNK
    return pl.pallas_call(
        _pool_kernel,
        grid=(bsz, lp // TM_POOL),
        in_specs=[
            pl.BlockSpec((1, TM_POOL, D_MODEL), lambda b, i: (b, i, COL_POOL // D_MODEL)),
            pl.BlockSpec((1, CHUNK, D_MODEL),
                         lambda b, i: (b, jnp.maximum(i * rb - 1, 0), COL_POOL // D_MODEL)),
            pl.BlockSpec((POOL_GROUPS, POOL_GROUP_DIM, POOL_GROUP_DIM), lambda b, i: (0, 0, 0)),
            pl.BlockSpec((None, 1, D_MODEL), lambda b, i: (layer, 0, 0)),
            pl.BlockSpec((D_MODEL, D_MODEL), lambda b, i: (0, 0)),
        ],
        out_specs=pl.BlockSpec((1, TM_POOL, D_MODEL), lambda b, i: (b, i, 0)),
        out_shape=jax.ShapeDtypeStruct((bsz, lp, D_MODEL), bf16),
        scratch_shapes=[
            pltpu.VMEM((TM_POOL, D_MODEL), bf16),
            pltpu.VMEM((TM_POOL, D_MODEL), bf16),
        ],
        compiler_params=pltpu.CompilerParams(
            dimension_semantics=("parallel", "parallel"),
            vmem_limit_bytes=VMEM_LIMIT),
        name="pool",
    )(proj, proj, wg, scale, wup)


def _conv_silu(halo, cur, w, b, use_halo):
    n = cur.shape[1]
    tail = CONV_WIDTH - 1
    x3 = cur.astype(f32).reshape(REGS, SUBLANES, n)
    prev_tail = halo[CONV_HALO - tail * SUBLANES:, :].astype(f32)
    prev_tail = jnp.where(use_halo, prev_tail, 0.0).reshape(tail, SUBLANES, n)
    merged = jnp.concatenate([x3[REGS - tail:, :SUBLANES - 1, :], prev_tail[:, SUBLANES - 1:, :]], axis=1)
    wrapped = pltpu.roll(merged, 1, axis=1)
    acc = b.reshape(1, 1, n) + x3 * w[CONV_WIDTH - 1:CONV_WIDTH, :].reshape(1, 1, n)
    for d in range(1, CONV_WIDTH):
        shifted = jnp.concatenate([wrapped[tail - d:], x3[:REGS - d]], axis=0)
        acc = acc + shifted * w[CONV_WIDTH - 1 - d:CONV_WIDTH - d, :].reshape(1, 1, n)
    return _silu_of_twice(acc).reshape(CHUNK, n)


def _chunk_has_predecessor(k, seq_start):
    return jnp.logical_not(jnp.logical_and(seq_start, k == 0))


def _ssd_group_front(k, g, seq_start, ssd_refs):
    (xs_ref, xs_h_ref, b_ref, b_h_ref, c_ref, c_h_ref, _, _, cw_ref, cb_ref) = ssd_refs[:10]
    rows = slice(k * CHUNK, (k + 1) * CHUNK)
    use_halo = _chunk_has_predecessor(k, seq_start)

    def conv_of(cur_ref, halo_ref, cols, wcols):
        if k == 0:
            halo = halo_ref[0, :, cols]
        else:
            halo = cur_ref[0, k * CHUNK - CONV_HALO:k * CHUNK, cols]
        return _conv_silu(halo, cur_ref[0, rows, cols], cw_ref[:, wcols], cb_ref[:, wcols], use_halo)

    xcols = slice(g * GROUP_X, (g + 1) * GROUP_X)
    ncols = slice(g * D_STATE, (g + 1) * D_STATE)
    bc_b = slice(D_INNER + g * D_STATE, D_INNER + (g + 1) * D_STATE)
    bc_c = slice(D_INNER + SSM_GROUPS * D_STATE + g * D_STATE,
                 D_INNER + SSM_GROUPS * D_STATE + (g + 1) * D_STATE)
    xg = conv_of(xs_ref, xs_h_ref, xcols, xcols)
    bg = conv_of(b_ref, b_h_ref, ncols, bc_b).astype(bf16)
    cg = conv_of(c_ref, c_h_ref, ncols, bc_c)
    cb = jax.lax.dot_general(cg.astype(bf16), bg, NT_DIMS, preferred_element_type=f32)
    return xg, bg, cg, cb


def _ssd_prologue(k, seq_start, ssd_refs):
    dt_ref, dtb_ref, alog_ref, eexp_ref = ssd_refs[7], ssd_refs[10], ssd_refs[11], ssd_refs[14]
    rows = slice(k * CHUNK, (k + 1) * CHUNK)
    use_halo = _chunk_has_predecessor(k, seq_start)

    tok = _token_of(jax.lax.broadcasted_iota(jnp.int32, (1, CHUNK), 1))
    valid_t = jnp.logical_or(use_halo, tok >= PAD_FRONT)
    raw_t = (dt_ref[0, rows, :] + dtb_ref[...]).T[0:SSM_HEADS, :]
    dt_t = jnp.where(valid_t, _softplus(raw_t), 0.0)
    a_col = -jnp.exp(alog_ref[...])
    tok_r = _token_of(jax.lax.broadcasted_iota(jnp.int32, (CHUNK, CHUNK), 0))
    tok_c = _token_of(jax.lax.broadcasted_iota(jnp.int32, (CHUNK, CHUNK), 1))
    causal = tok_r >= tok_c
    tri_t = jnp.where(tok_r <= tok_c, 1.0, 0.0).astype(bf16)
    acs_t = _dot_exact_rhs(dt_t * a_col, tri_t)
    yield
    acs_last = acs_t[:, CHUNK - 1:CHUNK]
    rowarg_t = acs_t * LOG2E - jnp.log(dt_t) * LOG2E
    w_state_t = jnp.exp(acs_last - acs_t) * dt_t
    e_acs_last = jnp.exp(acs_last)

    def tokens_on_rows(x):
        pad = jnp.zeros((CHUNK - SSM_HEADS, CHUNK), f32)
        return jnp.concatenate([x, pad], axis=0).T

    acs2 = tokens_on_rows(acs_t * LOG2E)
    w_exp = _dot(tokens_on_rows(w_state_t).astype(bf16), eexp_ref[...])
    yield
    chunk_decay =tokens_on_rows(jnp.broadcast_to(e_acs_last, (SSM_HEADS, CHUNK)))[0:SUBLANES, :]
    cd_exp = _dot_exact_rhs(chunk_decay, eexp_ref[...])[0:1, :]

    return acs2, rowarg_t, w_exp, cd_exp, causal


def _ssd_group_lhs(g, front, prologue):
    _, _, cg, cb = front
    acs2, rowarg_t, _, _, causal = prologue
    lhs = []
    for r in range(HEADS_PER_GROUP):
        h = g * HEADS_PER_GROUP + r
        col = jnp.broadcast_to(acs2[:, h:h + 1], (CHUNK, CHUNK))
        lm = jnp.exp2(col - rowarg_t[h:h + 1, :])
        m = (jnp.where(causal, lm, 0.0) * cb).astype(bf16)
        ce = (cg * jnp.exp2(col)).astype(bf16)
        lhs.append(jnp.concatenate([m, ce], axis=1))
    return lhs


def _ssd_group_back(k, g, front, lhs, prologue, ssd_refs, state_scr, y_ref):
    z_ref, dskip_ref, nw_ref = ssd_refs[6], ssd_refs[12], ssd_refs[13]
    xg, bg, _, _ = front
    _, _, w_exp, cd_exp, _ = prologue
    rows = slice(k * CHUNK, (k + 1) * CHUNK)
    xcols = slice(g * GROUP_X, (g + 1) * GROUP_X)
    low_head = jax.lax.broadcasted_iota(jnp.int32, (1, LANES), 1) < SSM_HEAD_DIM
    xg_b = xg.astype(bf16)
    s_old = state_scr[g]
    s_b = s_old.astype(bf16)

    halves = []
    for half in range(HEADS_PER_GROUP // 2):
        hcols = slice(half * LANES, (half + 1) * LANES)
        rhs = jnp.concatenate([xg_b[:, hcols], s_b[:, hcols]], axis=0)
        outs = [_dot(lhs[r], rhs) for r in (2 * half, 2 * half + 1)]
        halves.append(jnp.where(low_head, outs[0], outs[1]))
    y = jnp.concatenate(halves, axis=1)

    xw = (xg * w_exp[:, xcols]).astype(bf16)
    st = jax.lax.dot_general(bg, xw, (((0,), (0,)), ((), ())), preferred_element_type=f32)
    state_scr[g] = s_old * cd_exp[:, xcols] + st

    y = y + xg * dskip_ref[:, xcols]
    y = y * _silu_of_twice(z_ref[0, rows, xcols].astype(f32))
    ms = jnp.mean(y * y, axis=-1, keepdims=True)
    y = y * jax.lax.rsqrt(ms + EPS) * nw_ref[:, xcols]
    y_ref[rows, xcols] = y.astype(bf16)


def _ssd_tile(seq_start, ssd_refs, state_scr, y_ref):
    n_chunks = TM_MLP // CHUNK
    items = [(k, g) for k in range(n_chunks) for g in range(SSM_GROUPS)]

    def advance(k):
        try:
            next(pending_prologue[k])
        except StopIteration as done:
            prologues[k] = done.value

    prologues, fronts = {}, {}
    pending_prologue = {k: _ssd_prologue(k, seq_start, ssd_refs) for k in range(n_chunks)}
    for _ in range(PROLOGUE_PARTS - 1):
        advance(0)
        yield
    advance(0)
    fronts[items[0]] = _ssd_group_front(*items[0], seq_start, ssd_refs)
    yield
    for n, (k, g) in enumerate(items):
        if n + 1 < len(items):
            fronts[items[n + 1]] = _ssd_group_front(*items[n + 1], seq_start, ssd_refs)
        if g in NEXT_PROLOGUE_AT and k + 1 < n_chunks:
            advance(k + 1)
        front = fronts.pop((k, g))
        lhs = _ssd_group_lhs(g, front, prologues[k])
        _ssd_group_back(k, g, front, lhs, prologues[k], ssd_refs, state_scr, y_ref)
        yield


def _merge_mlp_body(h_ref, gate_ref, ypool_ref, ynorm_ref, bgate_ref, wssd_ref, wo_ref,
                    mw_ref, ff1_ref, ff2_ref, fw_ref, o_ref, final):
    def dot_in_pieces(a, w_ref, row0, col0, n_cols, col_pieces):
        kw = a.shape[1] // MERGE_K_PIECES
        cw = n_cols // col_pieces
        out = []
        for p in range(col_pieces):
            acc = None
            for q in range(MERGE_K_PIECES):
                part = _dot(a[:, q * kw:(q + 1) * kw],
                            w_ref[row0 + q * kw:row0 + (q + 1) * kw, col0 + p * cw:col0 + (p + 1) * cw])
                acc = part if acc is None else acc + part
                yield
            out.append(acc)
        return jnp.concatenate(out, axis=1)

    y_ssd = yield from dot_in_pieces(ynorm_ref[...], wssd_ref, 0, 0, D_MODEL, 4)
    gates = _sigmoid(gate_ref[0].astype(f32) + bgate_ref[...])
    mix = (gates[:, :D_MODEL] * ypool_ref[0].astype(f32) + gates[:, D_MODEL:] * y_ssd).astype(bf16)
    h1 = h_ref[0] + (yield from dot_in_pieces(mix, wo_ref, 0, 0, D_MODEL, 2))
    ms = jnp.mean(h1 * h1, axis=-1, keepdims=True)
    v = (h1 * jax.lax.rsqrt(ms + EPS) * mw_ref[...]).astype(bf16)
    acc = h1
    for k in range(D_FF // FF_CHUNK):
        hid = jnp.maximum((yield from dot_in_pieces(v, ff1_ref, 0, k * FF_CHUNK, FF_CHUNK, 2)), 0.0)
        hid = (hid * hid).astype(bf16)
        acc = acc + (yield from dot_in_pieces(hid, ff2_ref, k * FF_CHUNK, 0, D_MODEL, 2))
    if final:
        ms = jnp.mean(acc * acc, axis=-1, keepdims=True)
        acc = acc * jax.lax.rsqrt(ms + EPS) * fw_ref[...]
    o_ref[0] = acc


N_SSD_REFS = 15
SCAN_STAGES_PER_MERGE_STAGE = 1
MERGE_K_PIECES = 1
PROLOGUE_PARTS = 3
NEXT_PROLOGUE_AT = (2, 4, 6)
_DONE = object()


def _mixer_mlp_kernel(*refs, final):
    ssd_refs = refs[:N_SSD_REFS]
    (h_ref, gate_ref, ypool_ref, bgate_ref, wssd_ref, wo_ref, mw_ref, ff1_ref, ff2_ref, fw_ref,
     o_ref, state_scr, ybuf) = refs[N_SSD_REFS:]
    s = pl.program_id(1)

    @pl.when(s == 0)
    def _():
        state_scr[...] = jnp.zeros_like(state_scr)
        ybuf[...] = jnp.zeros_like(ybuf)

    write_slot = jnp.bitwise_and(s, 1)
    merge = _merge_mlp_body(h_ref, gate_ref, ypool_ref, ybuf.at[1 - write_slot], bgate_ref, wssd_ref,
                            wo_ref, mw_ref, ff1_ref, ff2_ref, fw_ref, o_ref, final)
    scan = _ssd_tile(s == 0, ssd_refs, state_scr, ybuf.at[write_slot])
    pending = [scan, merge]
    while pending:
        for gen, stages in ((scan, SCAN_STAGES_PER_MERGE_STAGE), (merge, 1)):
            for _ in range(stages):
                if gen in pending and next(gen, _DONE) is _DONE:
                    pending.remove(gen)


def _mixer_mlp(layer, h, proj, dt_raw, ypool, cw, cb, dtb, alog, dskip, ssd_nw, eexp,
               bgate, wssd, wo, mw, ff1, ff2, fw):
    bsz, lp, _ = h.shape
    n_tiles = lp // TM_MLP
    hb = TM_MLP // CONV_HALO

    def scan_tile(b, s):
        return jnp.minimum(s, n_tiles - 1)

    def merge_tile(b, s):
        return jnp.maximum(s - 1, 0)

    def cur(width, col):
        return pl.BlockSpec((1, TM_MLP, width), lambda b, s: (b, scan_tile(b, s), col // width))

    def halo(width, col):
        return pl.BlockSpec((1, CONV_HALO, width),
                            lambda b, s: (b, jnp.maximum(scan_tile(b, s) * hb - 1, 0), col // width))

    def const(shape):
        return pl.BlockSpec((None,) + shape, lambda b, s: (layer,) + (0,) * len(shape),
                            pipeline_mode=pl.Buffered(1))

    def resident(shape):
        return pl.BlockSpec(shape, lambda b, s: (0,) * len(shape), pipeline_mode=pl.Buffered(1))

    def merge_in(width, col):
        return pl.BlockSpec((1, TM_MLP, width), lambda b, s: (b, merge_tile(b, s), col // width))

    n_bc = SSM_GROUPS * D_STATE
    return pl.pallas_call(
        functools.partial(_mixer_mlp_kernel, final=layer == DEPTH - 1),
        grid=(bsz, n_tiles + 1),
        in_specs=[
            cur(D_INNER, COL_XS), halo(D_INNER, COL_XS),
            cur(n_bc, COL_B), halo(n_bc, COL_B),
            cur(n_bc, COL_C), halo(n_bc, COL_C),
            cur(D_INNER, COL_Z),
            pl.BlockSpec((1, TM_MLP, DT_LANES), lambda b, s: (b, scan_tile(b, s), 0)),
            const((CONV_WIDTH, D_INNER + 2 * n_bc)),
            const((1, D_INNER + 2 * n_bc)),
            const((1, DT_LANES)),
            const((SSM_HEADS, 1)),
            const((1, D_INNER)),
            const((1, D_INNER)),
            resident((DT_LANES, D_INNER)),
            merge_in(D_MODEL, 0),
            merge_in(2 * D_MODEL, COL_GATE),
            merge_in(D_MODEL, 0),
            const((1, 2 * D_MODEL)),
            resident((D_INNER, D_MODEL)),
            resident((D_MODEL, D_MODEL)),
            const((1, D_MODEL)),
            resident((D_MODEL, D_FF)),
            resident((D_FF, D_MODEL)),
            resident((1, D_MODEL)),
        ],
        out_specs=pl.BlockSpec((1, TM_MLP, D_MODEL), lambda b, s: (b, merge_tile(b, s), 0)),
        out_shape=jax.ShapeDtypeStruct((bsz, lp, D_MODEL), f32),
        scratch_shapes=[
            pltpu.VMEM((SSM_GROUPS, D_STATE, GROUP_X), f32),
            pltpu.VMEM((2, TM_MLP, D_INNER), bf16),
        ],
        compiler_params=pltpu.CompilerParams(
            dimension_semantics=("parallel", "arbitrary"),
            vmem_limit_bytes=MIXER_VMEM_LIMIT),
        name="mixer_mlp",
    )(proj, proj, proj, proj, proj, proj, proj, dt_raw, cw, cb, dtb, alog, dskip, ssd_nw, eexp,
      h, proj, ypool, bgate, wssd, wo, mw, ff1, ff2, fw)


def _to_strided_rows(t):
    bsz, n, d = t.shape
    t = t.reshape(bsz, n // CHUNK, SUBLANES, REGS, d)
    return jnp.swapaxes(t, 2, 3).reshape(bsz, n, d)


def _from_strided_rows(t):
    bsz, n, d = t.shape
    t = t.reshape(bsz, n // CHUNK, REGS, SUBLANES, d)
    return jnp.swapaxes(t, 2, 3).reshape(bsz, n, d)


def kernel(x, meta_tokens, mix_norm_w, w_in, b_gate, pool_w_group, pool_scale, w_pool_up,
           conv_w, conv_b, dt_bias, a_log, d_skip, ssd_norm_w, w_ssd_out, w_o,
           mlp_norm_w, w_ff1, w_ff2, final_norm_w):
    bsz, seq, _ = x.shape
    meta = jnp.broadcast_to(meta_tokens[None].astype(x.dtype), (bsz, N_META, D_MODEL))
    first_chunk = jnp.concatenate([jnp.zeros((bsz, PAD_FRONT, D_MODEL), x.dtype), meta], axis=1)
    h = _to_strided_rows(jnp.concatenate([first_chunk, x], axis=1))

    w_t = jnp.swapaxes(w_in, 1, 2)
    row = lambda p: p[:, None, :]
    dtb = row(jnp.pad(dt_bias, ((0, 0), (0, DT_LANES - SSM_HEADS))))
    dskip = row(jnp.repeat(d_skip, SSM_HEAD_DIM, axis=-1))
    eexp = (jnp.arange(DT_LANES)[:, None] == (jnp.arange(D_INNER)[None, :] // SSM_HEAD_DIM)).astype(bf16)
    cw = conv_w * 0.5
    cb = row(conv_b * 0.5)
    alog = a_log[:, :, None]
    mix_nw, pscale, ssd_nw = row(mix_norm_w), row(pool_scale), row(ssd_norm_w)
    bgate, mlp_nw = row(b_gate), row(mlp_norm_w)
    to_cast = (w_ff1, w_ff2, w_ssd_out, w_o, w_pool_up, pool_w_group)

    for i in range(DEPTH):
        proj, dt_raw, (ff1, ff2, wssd, wo, wup, wg) = _in_proj(i, h, mix_nw, w_t, to_cast)
        ypool = _pool(i, proj, wg, pscale, wup)
        h = _mixer_mlp(i, h, proj, dt_raw, ypool, cw, cb, dtb, alog, dskip, ssd_nw, eexp,
                       bgate, wssd, wo, mlp_nw, ff1, ff2, final_norm_w[None])
    return _from_strided_rows(h)[:, h.shape[1] - seq:]
```

```python
import functools
import itertools
import math

import jax
import jax.numpy as jnp
from jax.experimental import pallas as pl
from jax.experimental.pallas import tpu as pltpu

D_MODEL = 1024
DEPTH = 4
N_META = 16
POOL_GROUPS = 4
POOL_GROUP_DIM = 256
POOL_WINDOWS = (2, 4, 8, 16)
D_INNER = 2048
SSM_HEAD_DIM = 64
SSM_HEADS = 32
SSM_GROUPS = 8
HEADS_PER_GROUP = 4
D_STATE = 128
CONV_WIDTH = 4
CHUNK = 128
D_FF = 4096
EPS = 1e-5
LOG2E = math.log2(math.e)

OFF_DT = 7168
OFF_GATE = 7200
W_BLOCK = 1024
WBLK_POOL, WBLK_Z, WBLK_XS, WBLK_B, WBLK_C = 0, 1, 3, 5, 6

COL_Z = 0
COL_XS = 2048
COL_GATE = 4096
COL_POOL = 6144
COL_B = 7168
COL_C = 8192
N_MAIN = 9216
DT_LANES = 128

PAD_FRONT = CHUNK - N_META
SUBLANES = 8
LANES = 128
REGS = CHUNK // SUBLANES
CONV_HALO = 32
GROUP_X = HEADS_PER_GROUP * SSM_HEAD_DIM

TM_PROJ = 1408
TM_POOL = 384
TM_MLP = 384
FF_CHUNK = 1024
VMEM_LIMIT = 56 * 1024 * 1024
MIXER_VMEM_LIMIT = 60 * 1024 * 1024

f32 = jnp.float32
bf16 = jnp.bfloat16


def _dot(a, b):
    return jnp.dot(a, b, preferred_element_type=f32)


def _token_of(q):
    return jnp.right_shift(q, 3) + jnp.bitwise_and(q, SUBLANES - 1) * REGS


def _split3(x):
    p1 = x.astype(bf16)
    r1 = x - p1.astype(f32)
    p2 = r1.astype(bf16)
    p3 = (r1 - p2.astype(f32)).astype(bf16)
    return p1, p2, p3


def _dot_exact_rhs(x, e):
    p1, p2, p3 = _split3(x)
    return _dot(p1, e) + _dot(p2, e) + _dot(p3, e)


def _sigmoid(x):
    return 1.0 / (1.0 + jnp.exp(-x))


def _silu_of_twice(hx):
    return hx + hx * jnp.tanh(hx)


def _softplus(x):
    return jnp.maximum(x, 0.0) + jnp.log1p(jnp.exp(-jnp.abs(x)))


GATE_BLK0 = COL_GATE // W_BLOCK
NT_DIMS = (((1,), (1,)), ((), ()))


CAST_PLAN = (
    ((32, D_FF), D_MODEL // 32),
    ((128, D_MODEL), D_FF // 128),
    ((128, D_MODEL), D_INNER // 128),
    ((128, D_MODEL), D_MODEL // 128),
    ((128, D_MODEL), D_MODEL // 128),
    ((POOL_GROUPS, POOL_GROUP_DIM, POOL_GROUP_DIM), 1),
)
CAST_STARTS = tuple(sum(n for _, n in CAST_PLAN[:p]) for p in range(len(CAST_PLAN)))
N_CAST = len(CAST_PLAN)


def _in_proj_kernel(h_ref, nw_ref, w_ref, wdt_ref, *rest):
    cast_in = rest[:N_CAST]
    o_ref, dt_ref = rest[N_CAST:N_CAST + 2]
    cast_out = rest[N_CAST + 2:2 * N_CAST + 2]
    u_scr = rest[-1]
    j = pl.program_id(1)
    i = pl.program_id(2)
    rows = pl.ds(pl.multiple_of(i * TM_PROJ, CHUNK), TM_PROJ)

    step = (pl.program_id(0) * pl.num_programs(1) + j) * pl.num_programs(2) + i
    for src, dst, start, (_, n_blocks) in zip(cast_in, cast_out, CAST_STARTS, CAST_PLAN):
        @pl.when(jnp.logical_and(step >= start, step < start + n_blocks))
        def _(src=src, dst=dst):
            dst[...] = src[...].astype(bf16)

    @pl.when(j == 0)
    def _():
        x = h_ref[0]
        ms = jnp.mean(x * x, axis=-1, keepdims=True)
        u = x * jax.lax.rsqrt(ms + EPS) * nw_ref[...]
        r = i * TM_PROJ + jax.lax.broadcasted_iota(jnp.int32, (TM_PROJ, 1), 0)
        valid = jnp.logical_or(r >= CHUNK, jnp.bitwise_and(r, SUBLANES - 1) == SUBLANES - 1)
        u = jnp.where(valid, u, 0.0).astype(bf16)
        u_scr[rows, :] = u
        dt_ref[0] = jax.lax.dot_general(u, wdt_ref[0].astype(bf16), NT_DIMS, preferred_element_type=f32)

    w = w_ref[0].astype(bf16)
    acc = jax.lax.dot_general(u_scr[rows, :], w, NT_DIMS, preferred_element_type=f32)
    o_ref[0] = (acc * jnp.where(j < COL_XS // W_BLOCK, 0.5, 1.0)).astype(bf16)


def _w_row_of(j):
    blk = jnp.where(j < GATE_BLK0, j + WBLK_Z, jnp.where(j == COL_POOL // W_BLOCK, WBLK_POOL, j - 2))
    is_gate = jnp.logical_and(j >= GATE_BLK0, j < GATE_BLK0 + 2)
    return pl.multiple_of(jnp.where(is_gate, OFF_GATE + (j - GATE_BLK0) * W_BLOCK, blk * W_BLOCK), SUBLANES)


def _in_proj(layer, h, nw, w_t, cast_weights):
    bsz, lp, _ = h.shape
    n_i = lp // TM_PROJ
    n_j = N_MAIN // W_BLOCK
    grid = (bsz, n_j, n_i)
    assert CAST_STARTS[-1] + CAST_PLAN[-1][1] <= bsz * n_j * n_i

    def first_pass_tile(b, j, i):
        return (b, jnp.where(j == 0, i, n_i - 1), 0)

    def cast_block(start, n_blocks):
        return lambda b, j, i: jnp.clip((b * n_j + j) * n_i + i - start, 0, n_blocks - 1)

    cast_in, cast_out, cast_shapes = [], [], []
    for w, start, (blk, n_blocks) in zip(cast_weights, CAST_STARTS, CAST_PLAN):
        which = cast_block(start, n_blocks)
        tail = (0,) * (len(blk) - 1)
        cast_in.append(pl.BlockSpec((None,) + blk, lambda b, j, i, which=which, tail=tail:
                                    (layer, which(b, j, i)) + tail))
        cast_out.append(pl.BlockSpec(blk, lambda b, j, i, which=which, tail=tail: (which(b, j, i),) + tail))
        cast_shapes.append(jax.ShapeDtypeStruct(w.shape[1:], bf16))

    elem = lambda *shape: tuple(pl.Element(s) for s in shape)
    proj, dt_raw, *casts = pl.pallas_call(
        _in_proj_kernel,
        grid=grid,
        in_specs=[
            pl.BlockSpec((1, TM_PROJ, D_MODEL), first_pass_tile),
            pl.BlockSpec((None, 1, D_MODEL), lambda b, j, i: (layer, 0, 0)),
            pl.BlockSpec(elem(1, W_BLOCK, D_MODEL), lambda b, j, i: (layer, _w_row_of(j), 0)),
            pl.BlockSpec(elem(1, DT_LANES, D_MODEL), lambda b, j, i: (layer, OFF_DT, 0)),
        ] + cast_in,
        out_specs=[
            pl.BlockSpec((1, TM_PROJ, W_BLOCK), lambda b, j, i: (b, i, j)),
            pl.BlockSpec((1, TM_PROJ, DT_LANES), first_pass_tile),
        ] + cast_out,
        out_shape=[
            jax.ShapeDtypeStruct((bsz, lp, N_MAIN), bf16),
            jax.ShapeDtypeStruct((bsz, lp, DT_LANES), f32),
        ] + cast_shapes,
        scratch_shapes=[pltpu.VMEM((lp, D_MODEL), bf16)],
        compiler_params=pltpu.CompilerParams(
            dimension_semantics=("arbitrary", "arbitrary", "arbitrary"),
            vmem_limit_bytes=VMEM_LIMIT),
        name="in_proj",
    )(h, nw, w_t, w_t, *cast_weights)
    return proj, dt_raw, casts


def _pool_kernel(u_ref, prev_ref, wg_ref, scale_ref, wup_ref, o_ref, pooled_scr, y1_scr):
    i = pl.program_id(1)
    prev_chunk = prev_ref[0]
    prev_chunk = jnp.where(i > 0, prev_chunk, jnp.zeros_like(prev_chunk))

    q_io = jax.lax.broadcasted_iota(jnp.int32, (CHUNK, 2 * CHUNK), 0)
    k_io = jax.lax.broadcasted_iota(jnp.int32, (CHUNK, 2 * CHUNK), 1)
    k_tok = _token_of(jnp.bitwise_and(k_io, CHUNK - 1)) + jnp.right_shift(k_io, 7) * CHUNK
    delta = _token_of(q_io) + CHUNK - k_tok
    tok_in_chunk = _token_of(jax.lax.broadcasted_iota(jnp.int32, (CHUNK, 1), 0))
    for g, win in enumerate(POOL_WINDOWS):
        band = jnp.where(jnp.logical_and(delta >= 0, delta < win), 1.0, 0.0).astype(bf16)
        cols = slice(g * POOL_GROUP_DIM, (g + 1) * POOL_GROUP_DIM)
        for sb in range(TM_POOL // CHUNK):
            rows = slice(sb * CHUNK, (sb + 1) * CHUNK)
            before = prev_chunk[:, cols] if sb == 0 else u_ref[0, (sb - 1) * CHUNK:sb * CHUNK, cols]
            cur = u_ref[0, rows, cols]
            wsum = _dot(band, jnp.concatenate([before, cur], axis=0))
            tok = i * TM_POOL + sb * CHUNK + tok_in_chunk - PAD_FRONT
            cnt = jnp.clip(tok + 1, 1, win).astype(f32)
            pooled_scr[rows, cols] = (wsum / cnt - cur.astype(f32)).astype(bf16)
        y1 = _dot(pooled_scr[:, cols], wg_ref[g]) * scale_ref[:, cols]
        y1_scr[:, cols] = y1.astype(bf16)
    o_ref[0] = _dot(y1_scr[...], wup_ref[...]).astype(bf16)


def _pool(layer, proj, wg, scale, wup):
    bsz, lp, _ = proj.shape
    rb = TM_POOL // CHUNK
    return pl.pallas_call(
        _pool_kernel,
        grid=(bsz, lp // TM_POOL),
        in_specs=[
            pl.BlockSpec((1, TM_POOL, D_MODEL), lambda b, i: (b, i, COL_POOL // D_MODEL)),
            pl.BlockSpec((1, CHUNK, D_MODEL),
                         lambda b, i: (b, jnp.maximum(i * rb - 1, 0), COL_POOL // D_MODEL)),
            pl.BlockSpec((POOL_GROUPS, POOL_GROUP_DIM, POOL_GROUP_DIM), lambda b, i: (0, 0, 0)),
            pl.BlockSpec((None, 1, D_MODEL), lambda b, i: (layer, 0, 0)),
            pl.BlockSpec((D_MODEL, D_MODEL), lambda b, i: (0, 0)),
        ],
        out_specs=pl.BlockSpec((1, TM_POOL, D_MODEL), lambda b, i: (b, i, 0)),
        out_shape=jax.ShapeDtypeStruct((bsz, lp, D_MODEL), bf16),
        scratch_shapes=[
            pltpu.VMEM((TM_POOL, D_MODEL), bf16),
            pltpu.VMEM((TM_POOL, D_MODEL), bf16),
        ],
        compiler_params=pltpu.CompilerParams(
            dimension_semantics=("parallel", "parallel"),
            vmem_limit_bytes=VMEM_LIMIT),
        name="pool",
    )(proj, proj, wg, scale, wup)


def _conv_silu(halo, cur, w, b, use_halo):
    n = cur.shape[1]
    tail = CONV_WIDTH - 1
    x3 = cur.astype(f32).reshape(REGS, SUBLANES, n)
    prev_tail = halo[CONV_HALO - tail * SUBLANES:, :].astype(f32)
    prev_tail = jnp.where(use_halo, prev_tail, 0.0).reshape(tail, SUBLANES, n)
    merged = jnp.concatenate([x3[REGS - tail:, :SUBLANES - 1, :], prev_tail[:, SUBLANES - 1:, :]], axis=1)
    wrapped = pltpu.roll(merged, 1, axis=1)
    acc = b.reshape(1, 1, n) + x3 * w[CONV_WIDTH - 1:CONV_WIDTH, :].reshape(1, 1, n)
    for d in range(1, CONV_WIDTH):
        shifted = jnp.concatenate([wrapped[tail - d:], x3[:REGS - d]], axis=0)
        acc = acc + shifted * w[CONV_WIDTH - 1 - d:CONV_WIDTH - d, :].reshape(1, 1, n)
    return _silu_of_twice(acc).reshape(CHUNK, n)


def _chunk_has_predecessor(k, seq_start):
    return jnp.logical_not(jnp.logical_and(seq_start, k == 0))


def _ssd_group_front(k, g, seq_start, ssd_refs):
    (xs_ref, xs_h_ref, b_ref, b_h_ref, c_ref, c_h_ref, _, _, cw_ref, cb_ref) = ssd_refs[:10]
    rows = slice(k * CHUNK, (k + 1) * CHUNK)
    use_halo = _chunk_has_predecessor(k, seq_start)

    def conv_of(cur_ref, halo_ref, cols, wcols):
        if k == 0:
            halo = halo_ref[0, :, cols]
        else:
            halo = cur_ref[0, k * CHUNK - CONV_HALO:k * CHUNK, cols]
        return _conv_silu(halo, cur_ref[0, rows, cols], cw_ref[:, wcols], cb_ref[:, wcols], use_halo)

    xcols = slice(g * GROUP_X, (g + 1) * GROUP_X)
    ncols = slice(g * D_STATE, (g + 1) * D_STATE)
    bc_b = slice(D_INNER + g * D_STATE, D_INNER + (g + 1) * D_STATE)
    bc_c = slice(D_INNER + SSM_GROUPS * D_STATE + g * D_STATE,
                 D_INNER + SSM_GROUPS * D_STATE + (g + 1) * D_STATE)
    xg = conv_of(xs_ref, xs_h_ref, xcols, xcols)
    bg = conv_of(b_ref, b_h_ref, ncols, bc_b).astype(bf16)
    cg = conv_of(c_ref, c_h_ref, ncols, bc_c)
    cb = jax.lax.dot_general(cg.astype(bf16), bg, NT_DIMS, preferred_element_type=f32)
    return xg, bg, cg, cb


def _ssd_prologue(k, seq_start, ssd_refs):
    dt_ref, dtb_ref, alog_ref, eexp_ref = ssd_refs[7], ssd_refs[10], ssd_refs[11], ssd_refs[14]
    rows = slice(k * CHUNK, (k + 1) * CHUNK)
    use_halo = _chunk_has_predecessor(k, seq_start)

    tok = _token_of(jax.lax.broadcasted_iota(jnp.int32, (1, CHUNK), 1))
    valid_t = jnp.logical_or(use_halo, tok >= PAD_FRONT)
    raw_t = (dt_ref[0, rows, :] + dtb_ref[...]).T[0:SSM_HEADS, :]
    dt_t = jnp.where(valid_t, _softplus(raw_t), 0.0)
    a_col = -jnp.exp(alog_ref[...])
    tok_r = _token_of(jax.lax.broadcasted_iota(jnp.int32, (CHUNK, CHUNK), 0))
    tok_c = _token_of(jax.lax.broadcasted_iota(jnp.int32, (CHUNK, CHUNK), 1))
    causal = tok_r >= tok_c
    tri_t = jnp.where(tok_r <= tok_c, 1.0, 0.0).astype(bf16)
    acs_t = _dot_exact_rhs(dt_t * a_col, tri_t)
    yield
    acs_last = acs_t[:, CHUNK - 1:CHUNK]
    rowarg_t = acs_t * LOG2E - jnp.log(dt_t) * LOG2E
    w_state_t = jnp.exp(acs_last - acs_t) * dt_t
    e_acs_last = jnp.exp(acs_last)

    def tokens_on_rows(x):
        pad = jnp.zeros((CHUNK - SSM_HEADS, CHUNK), f32)
        return jnp.concatenate([x, pad], axis=0).T

    acs2 = tokens_on_rows(acs_t * LOG2E)
    w_exp = _dot(tokens_on_rows(w_state_t).astype(bf16), eexp_ref[...])
    yield
    chunk_decay =tokens_on_rows(jnp.broadcast_to(e_acs_last, (SSM_HEADS, CHUNK)))[0:SUBLANES, :]
    cd_exp = _dot_exact_rhs(chunk_decay, eexp_ref[...])[0:1, :]

    return acs2, rowarg_t, w_exp, cd_exp, causal


def _ssd_group_lhs(g, front, prologue):
    _, _, cg, cb = front
    acs2, rowarg_t, _, _, causal = prologue
    lhs = []
    for r in range(HEADS_PER_GROUP):
        h = g * HEADS_PER_GROUP + r
        col = jnp.broadcast_to(acs2[:, h:h + 1], (CHUNK, CHUNK))
        lm = jnp.exp2(col - rowarg_t[h:h + 1, :])
        m = (jnp.where(causal, lm, 0.0) * cb).astype(bf16)
        ce = (cg * jnp.exp2(col)).astype(bf16)
        lhs.append(jnp.concatenate([m, ce], axis=1))
    return lhs


def _ssd_group_back(k, g, front, lhs, prologue, ssd_refs, state_scr, y_ref):
    z_ref, dskip_ref, nw_ref = ssd_refs[6], ssd_refs[12], ssd_refs[13]
    xg, bg, _, _ = front
    _, _, w_exp, cd_exp, _ = prologue
    rows = slice(k * CHUNK, (k + 1) * CHUNK)
    xcols = slice(g * GROUP_X, (g + 1) * GROUP_X)
    low_head = jax.lax.broadcasted_iota(jnp.int32, (1, LANES), 1) < SSM_HEAD_DIM
    xg_b = xg.astype(bf16)
    s_old = state_scr[g]
    s_b = s_old.astype(bf16)

    halves = []
    for half in range(HEADS_PER_GROUP // 2):
        hcols = slice(half * LANES, (half + 1) * LANES)
        rhs = jnp.concatenate([xg_b[:, hcols], s_b[:, hcols]], axis=0)
        outs = [_dot(lhs[r], rhs) for r in (2 * half, 2 * half + 1)]
        halves.append(jnp.where(low_head, outs[0], outs[1]))
    y = jnp.concatenate(halves, axis=1)

    xw = (xg * w_exp[:, xcols]).astype(bf16)
    st = jax.lax.dot_general(bg, xw, (((0,), (0,)), ((), ())), preferred_element_type=f32)
    state_scr[g] = s_old * cd_exp[:, xcols] + st

    y = y + xg * dskip_ref[:, xcols]
    y = y * _silu_of_twice(z_ref[0, rows, xcols].astype(f32))
    ms = jnp.mean(y * y, axis=-1, keepdims=True)
    y = y * jax.lax.rsqrt(ms + EPS) * nw_ref[:, xcols]
    y_ref[rows, xcols] = y.astype(bf16)


def _ssd_tile(seq_start, ssd_refs, state_scr, y_ref):
    n_chunks = TM_MLP // CHUNK
    items = [(k, g) for k in range(n_chunks) for g in range(SSM_GROUPS)]

    def advance(k):
        try:
            next(pending_prologue[k])
        except StopIteration as done:
            prologues[k] = done.value

    prologues, fronts = {}, {}
    pending_prologue = {k: _ssd_prologue(k, seq_start, ssd_refs) for k in range(n_chunks)}
    for _ in range(PROLOGUE_PARTS - 1):
        advance(0)
        yield
    advance(0)
    fronts[items[0]] = _ssd_group_front(*items[0], seq_start, ssd_refs)
    yield
    for n, (k, g) in enumerate(items):
        if n + 1 < len(items):
            fronts[items[n + 1]] = _ssd_group_front(*items[n + 1], seq_start, ssd_refs)
        if g in NEXT_PROLOGUE_AT and k + 1 < n_chunks:
            advance(k + 1)
        front = fronts.pop((k, g))
        lhs = _ssd_group_lhs(g, front, prologues[k])
        _ssd_group_back(k, g, front, lhs, prologues[k], ssd_refs, state_scr, y_ref)
        yield


def _merge_mlp_body(h_ref, gate_ref, ypool_ref, ynorm_ref, bgate_ref, wssd_ref, wo_ref,
                    mw_ref, ff1_ref, ff2_ref, fw_ref, o_ref, final):
    def dot_in_pieces(a, w_ref, row0, col0, n_cols, col_pieces):
        kw = a.shape[1] // MERGE_K_PIECES
        cw = n_cols // col_pieces
        out = []
        for p in range(col_pieces):
            acc = None
            for q in range(MERGE_K_PIECES):
                part = _dot(a[:, q * kw:(q + 1) * kw],
                            w_ref[row0 + q * kw:row0 + (q + 1) * kw, col0 + p * cw:col0 + (p + 1) * cw])
                acc = part if acc is None else acc + part
                yield
            out.append(acc)
        return jnp.concatenate(out, axis=1)

    y_ssd = yield from dot_in_pieces(ynorm_ref[...], wssd_ref, 0, 0, D_MODEL, 4)
    gates = _sigmoid(gate_ref[0].astype(f32) + bgate_ref[...])
    mix = (gates[:, :D_MODEL] * ypool_ref[0].astype(f32) + gates[:, D_MODEL:] * y_ssd).astype(bf16)
    h1 = h_ref[0] + (yield from dot_in_pieces(mix, wo_ref, 0, 0, D_MODEL, 2))
    ms = jnp.mean(h1 * h1, axis=-1, keepdims=True)
    v = (h1 * jax.lax.rsqrt(ms + EPS) * mw_ref[...]).astype(bf16)
    acc = h1
    for k in range(D_FF // FF_CHUNK):
        hid = jnp.maximum((yield from dot_in_pieces(v, ff1_ref, 0, k * FF_CHUNK, FF_CHUNK, 2)), 0.0)
        hid = (hid * hid).astype(bf16)
        acc = acc + (yield from dot_in_pieces(hid, ff2_ref, k * FF_CHUNK, 0, D_MODEL, 2))
    if final:
        ms = jnp.mean(acc * acc, axis=-1, keepdims=True)
        acc = acc * jax.lax.rsqrt(ms + EPS) * fw_ref[...]
    o_ref[0] = acc


N_SSD_REFS = 15
SCAN_STAGES_PER_MERGE_STAGE = 1
MERGE_K_PIECES = 1
PROLOGUE_PARTS = 3
NEXT_PROLOGUE_AT = (2, 4, 6)
_DONE = object()


def _mixer_mlp_kernel(*refs, final, tiles_per_row):
    ssd_refs = refs[:N_SSD_REFS]
    (h_ref, gate_ref, ypool_ref, bgate_ref, wssd_ref, wo_ref, mw_ref, ff1_ref, ff2_ref, fw_ref,
     o_ref, state_scr, ybuf) = refs[N_SSD_REFS:]
    t = pl.program_id(0)
    last = pl.num_programs(0) - 1
    seq_start = jnp.logical_and(t % tiles_per_row == 0, t < last)

    @pl.when(seq_start)
    def _():
        state_scr[...] = jnp.zeros_like(state_scr)

    @pl.when(t == 0)
    def _():
        ybuf[...] = jnp.zeros_like(ybuf)

    write_slot = jnp.bitwise_and(t, 1)
    merge = _merge_mlp_body(h_ref, gate_ref, ypool_ref, ybuf.at[1 - write_slot], bgate_ref, wssd_ref,
                            wo_ref, mw_ref, ff1_ref, ff2_ref, fw_ref, o_ref, final)
    scan = _ssd_tile(seq_start, ssd_refs, state_scr, ybuf.at[write_slot])
    pending = [scan, merge]
    while pending:
        for gen, stages in ((scan, SCAN_STAGES_PER_MERGE_STAGE), (merge, 1)):
            for _ in range(stages):
                if gen in pending and next(gen, _DONE) is _DONE:
                    pending.remove(gen)


def _mixer_mlp(layer, h, proj, dt_raw, ypool, cw, cb, dtb, alog, dskip, ssd_nw, eexp,
               bgate, wssd, wo, mw, ff1, ff2, fw):
    bsz, lp, _ = h.shape
    n_tiles = lp // TM_MLP
    n_total = bsz * n_tiles
    hb = TM_MLP // CONV_HALO

    def scan_tile(t):
        tile = jnp.minimum(t, n_total - 1)
        return tile // n_tiles, tile % n_tiles

    def merge_tile(t):
        tile = jnp.maximum(t - 1, 0)
        return tile // n_tiles, tile % n_tiles

    def cur(width, col):
        return pl.BlockSpec((1, TM_MLP, width), lambda t: (*scan_tile(t), col // width))

    def halo(width, col):
        def index(t):
            b, i = scan_tile(t)
            return b, jnp.maximum(i * hb - 1, 0), col // width
        return pl.BlockSpec((1, CONV_HALO, width), index)

    def const(shape):
        return pl.BlockSpec((None,) + shape, lambda t: (layer,) + (0,) * len(shape),
                            pipeline_mode=pl.Buffered(1))

    def resident(shape):
        return pl.BlockSpec(shape, lambda t: (0,) * len(shape), pipeline_mode=pl.Buffered(1))

    def merge_in(width, col):
        return pl.BlockSpec((1, TM_MLP, width), lambda t: (*merge_tile(t), col // width))

    n_bc = SSM_GROUPS * D_STATE
    return pl.pallas_call(
        functools.partial(_mixer_mlp_kernel, final=layer == DEPTH - 1, tiles_per_row=n_tiles),
        grid=(n_total + 1,),
        in_specs=[
            cur(D_INNER, COL_XS), halo(D_INNER, COL_XS),
            cur(n_bc, COL_B), halo(n_bc, COL_B),
            cur(n_bc, COL_C), halo(n_bc, COL_C),
            cur(D_INNER, COL_Z),
            pl.BlockSpec((1, TM_MLP, DT_LANES), lambda t: (*scan_tile(t), 0)),
            const((CONV_WIDTH, D_INNER + 2 * n_bc)),
            const((1, D_INNER + 2 * n_bc)),
            const((1, DT_LANES)),
            const((SSM_HEADS, 1)),
            const((1, D_INNER)),
            const((1, D_INNER)),
            resident((DT_LANES, D_INNER)),
            merge_in(D_MODEL, 0),
            merge_in(2 * D_MODEL, COL_GATE),
            merge_in(D_MODEL, 0),
            const((1, 2 * D_MODEL)),
            resident((D_INNER, D_MODEL)),
            resident((D_MODEL, D_MODEL)),
            const((1, D_MODEL)),
            resident((D_MODEL, D_FF)),
            resident((D_FF, D_MODEL)),
            resident((1, D_MODEL)),
        ],
        out_specs=pl.BlockSpec((1, TM_MLP, D_MODEL), lambda t: (*merge_tile(t), 0)),
        out_shape=jax.ShapeDtypeStruct((bsz, lp, D_MODEL), f32),
        scratch_shapes=[
            pltpu.VMEM((SSM_GROUPS, D_STATE, GROUP_X), f32),
            pltpu.VMEM((2, TM_MLP, D_INNER), bf16),
        ],
        compiler_params=pltpu.CompilerParams(
            dimension_semantics=("arbitrary",),
            vmem_limit_bytes=MIXER_VMEM_LIMIT),
        name="mixer_mlp",
    )(proj, proj, proj, proj, proj, proj, proj, dt_raw, cw, cb, dtb, alog, dskip, ssd_nw, eexp,
      h, proj, ypool, bgate, wssd, wo, mw, ff1, ff2, fw)


def _to_strided_rows(t):
    bsz, n, d = t.shape
    t = t.reshape(bsz, n // CHUNK, SUBLANES, REGS, d)
    return jnp.swapaxes(t, 2, 3).reshape(bsz, n, d)


def _from_strided_rows(t):
    bsz, n, d = t.shape
    t = t.reshape(bsz, n // CHUNK, REGS, SUBLANES, d)
    return jnp.swapaxes(t, 2, 3).reshape(bsz, n, d)


def kernel(x, meta_tokens, mix_norm_w, w_in, b_gate, pool_w_group, pool_scale, w_pool_up,
           conv_w, conv_b, dt_bias, a_log, d_skip, ssd_norm_w, w_ssd_out, w_o,
           mlp_norm_w, w_ff1, w_ff2, final_norm_w):
    bsz, seq, _ = x.shape
    meta = jnp.broadcast_to(meta_tokens[None].astype(x.dtype), (bsz, N_META, D_MODEL))
    first_chunk = jnp.concatenate([jnp.zeros((bsz, PAD_FRONT, D_MODEL), x.dtype), meta], axis=1)
    h = _to_strided_rows(jnp.concatenate([first_chunk, x], axis=1))

    w_t = jnp.swapaxes(w_in, 1, 2)
    row = lambda p: p[:, None, :]
    dtb = row(jnp.pad(dt_bias, ((0, 0), (0, DT_LANES - SSM_HEADS))))
    dskip = row(jnp.repeat(d_skip, SSM_HEAD_DIM, axis=-1))
    eexp = (jnp.arange(DT_LANES)[:, None] == (jnp.arange(D_INNER)[None, :] // SSM_HEAD_DIM)).astype(bf16)
    cw = conv_w * 0.5
    cb = row(conv_b * 0.5)
    alog = a_log[:, :, None]
    mix_nw, pscale, ssd_nw = row(mix_norm_w), row(pool_scale), row(ssd_norm_w)
    bgate, mlp_nw = row(b_gate), row(mlp_norm_w)
    to_cast = (w_ff1, w_ff2, w_ssd_out, w_o, w_pool_up, pool_w_group)

    for i in range(DEPTH):
        proj, dt_raw, (ff1, ff2, wssd, wo, wup, wg) = _in_proj(i, h, mix_nw, w_t, to_cast)
        ypool = _pool(i, proj, wg, pscale, wup)
        h = _mixer_mlp(i, h, proj, dt_raw, ypool, cw, cb, dtb, alog, dskip, ssd_nw, eexp,
                       bgate, wssd, wo, mlp_nw, ff1, ff2, final_norm_w[None])
    return _from_strided_rows(h)[:, h.shape[1] - seq:]
```

```python
import functools
import itertools
import math

import jax
import jax.numpy as jnp
from jax.experimental import pallas as pl
from jax.experimental.pallas import tpu as pltpu

D_MODEL = 1024
DEPTH = 4
N_META = 16
POOL_GROUPS = 4
POOL_GROUP_DIM = 256
POOL_WINDOWS = (2, 4, 8, 16)
D_INNER = 2048
SSM_HEAD_DIM = 64
SSM_HEADS = 32
SSM_GROUPS = 8
HEADS_PER_GROUP = 4
D_STATE = 128
CONV_WIDTH = 4
CHUNK = 128
D_FF = 4096
EPS = 1e-5
LOG2E = math.log2(math.e)

OFF_DT = 7168
OFF_GATE = 7200
W_BLOCK = 1024
WBLK_POOL, WBLK_Z, WBLK_XS, WBLK_B, WBLK_C = 0, 1, 3, 5, 6

COL_Z = 0
COL_XS = 2048
COL_GATE = 4096
COL_POOL = 6144
COL_B = 7168
COL_C = 8192
N_MAIN = 9216
DT_LANES = 128

PAD_FRONT = CHUNK - N_META
SUBLANES = 8
LANES = 128
REGS = CHUNK // SUBLANES
CONV_HALO = 32
GROUP_X = HEADS_PER_GROUP * SSM_HEAD_DIM

TM_PROJ = 1408
TM_POOL = 1408
TM_MLP = 384
FF_CHUNK = 1024
VMEM_LIMIT = 56 * 1024 * 1024
MIXER_VMEM_LIMIT = 60 * 1024 * 1024

f32 = jnp.float32
bf16 = jnp.bfloat16


def _dot(a, b):
    return jnp.dot(a, b, preferred_element_type=f32)


def _token_of(q):
    return jnp.right_shift(q, 3) + jnp.bitwise_and(q, SUBLANES - 1) * REGS


def _split3(x):
    p1 = x.astype(bf16)
    r1 = x - p1.astype(f32)
    p2 = r1.astype(bf16)
    p3 = (r1 - p2.astype(f32)).astype(bf16)
    return p1, p2, p3


def _dot_exact_rhs(x, e):
    p1, p2, p3 = _split3(x)
    return _dot(p1, e) + _dot(p2, e) + _dot(p3, e)


def _sigmoid(x):
    return 1.0 / (1.0 + jnp.exp(-x))


def _silu_of_twice(hx):
    return hx + hx * jnp.tanh(hx)


def _softplus(x):
    return jnp.maximum(x, 0.0) + jnp.log1p(jnp.exp(-jnp.abs(x)))


GATE_BLK0 = COL_GATE // W_BLOCK
NT_DIMS = (((1,), (1,)), ((), ()))


CAST_PLAN = (
    ((32, D_FF), D_MODEL // 32),
    ((128, D_MODEL), D_FF // 128),
    ((128, D_MODEL), D_INNER // 128),
    ((128, D_MODEL), D_MODEL // 128),
    ((128, D_MODEL), D_MODEL // 128),
    ((POOL_GROUPS, POOL_GROUP_DIM, POOL_GROUP_DIM), 1),
)
CAST_STARTS = tuple(sum(n for _, n in CAST_PLAN[:p]) for p in range(len(CAST_PLAN)))
N_CAST = len(CAST_PLAN)


def _in_proj_kernel(h_ref, nw_ref, w_ref, wdt_ref, *rest):
    cast_in = rest[:N_CAST]
    o_ref, dt_ref = rest[N_CAST:N_CAST + 2]
    cast_out = rest[N_CAST + 2:2 * N_CAST + 2]
    u_scr = rest[-1]
    j = pl.program_id(1)
    i = pl.program_id(2)
    rows = pl.ds(pl.multiple_of(i * TM_PROJ, CHUNK), TM_PROJ)

    step = (pl.program_id(0) * pl.num_programs(1) + j) * pl.num_programs(2) + i
    for src, dst, start, (_, n_blocks) in zip(cast_in, cast_out, CAST_STARTS, CAST_PLAN):
        @pl.when(jnp.logical_and(step >= start, step < start + n_blocks))
        def _(src=src, dst=dst):
            dst[...] = src[...].astype(bf16)

    @pl.when(j == 0)
    def _():
        x = h_ref[0]
        ms = jnp.mean(x * x, axis=-1, keepdims=True)
        u = x * jax.lax.rsqrt(ms + EPS) * nw_ref[...]
        r = i * TM_PROJ + jax.lax.broadcasted_iota(jnp.int32, (TM_PROJ, 1), 0)
        valid = jnp.logical_or(r >= CHUNK, jnp.bitwise_and(r, SUBLANES - 1) == SUBLANES - 1)
        u = jnp.where(valid, u, 0.0).astype(bf16)
        u_scr[rows, :] = u
        dt_ref[0] = jax.lax.dot_general(u, wdt_ref[0].astype(bf16), NT_DIMS, preferred_element_type=f32)

    w = w_ref[0].astype(bf16)
    acc = jax.lax.dot_general(u_scr[rows, :], w, NT_DIMS, preferred_element_type=f32)
    o_ref[0] = (acc * jnp.where(j < COL_XS // W_BLOCK, 0.5, 1.0)).astype(bf16)


def _w_row_of(j):
    blk = jnp.where(j < GATE_BLK0, j + WBLK_Z, jnp.where(j == COL_POOL // W_BLOCK, WBLK_POOL, j - 2))
    is_gate = jnp.logical_and(j >= GATE_BLK0, j < GATE_BLK0 + 2)
    return pl.multiple_of(jnp.where(is_gate, OFF_GATE + (j - GATE_BLK0) * W_BLOCK, blk * W_BLOCK), SUBLANES)


def _in_proj(layer, h, nw, w_t, cast_weights):
    bsz, lp, _ = h.shape
    n_i = lp // TM_PROJ
    n_j = N_MAIN // W_BLOCK
    grid = (bsz, n_j, n_i)
    assert CAST_STARTS[-1] + CAST_PLAN[-1][1] <= bsz * n_j * n_i

    def first_pass_tile(b, j, i):
        return (b, jnp.where(j == 0, i, n_i - 1), 0)

    def cast_block(start, n_blocks):
        return lambda b, j, i: jnp.clip((b * n_j + j) * n_i + i - start, 0, n_blocks - 1)

    cast_in, cast_out, cast_shapes = [], [], []
    for w, start, (blk, n_blocks) in zip(cast_weights, CAST_STARTS, CAST_PLAN):
        which = cast_block(start, n_blocks)
        tail = (0,) * (len(blk) - 1)
        cast_in.append(pl.BlockSpec((None,) + blk, lambda b, j, i, which=which, tail=tail:
                                    (layer, which(b, j, i)) + tail))
        cast_out.append(pl.BlockSpec(blk, lambda b, j, i, which=which, tail=tail: (which(b, j, i),) + tail))
        cast_shapes.append(jax.ShapeDtypeStruct(w.shape[1:], bf16))

    elem = lambda *shape: tuple(pl.Element(s) for s in shape)
    proj, dt_raw, *casts = pl.pallas_call(
        _in_proj_kernel,
        grid=grid,
        in_specs=[
            pl.BlockSpec((1, TM_PROJ, D_MODEL), first_pass_tile),
            pl.BlockSpec((None, 1, D_MODEL), lambda b, j, i: (layer, 0, 0)),
            pl.BlockSpec(elem(1, W_BLOCK, D_MODEL), lambda b, j, i: (layer, _w_row_of(j), 0)),
            pl.BlockSpec(elem(1, DT_LANES, D_MODEL), lambda b, j, i: (layer, OFF_DT, 0)),
        ] + cast_in,
        out_specs=[
            pl.BlockSpec((1, TM_PROJ, W_BLOCK), lambda b, j, i: (b, i, j)),
            pl.BlockSpec((1, TM_PROJ, DT_LANES), first_pass_tile),
        ] + cast_out,
        out_shape=[
            jax.ShapeDtypeStruct((bsz, lp, N_MAIN), bf16),
            jax.ShapeDtypeStruct((bsz, lp, DT_LANES), f32),
        ] + cast_shapes,
        scratch_shapes=[pltpu.VMEM((lp, D_MODEL), bf16)],
        compiler_params=pltpu.CompilerParams(
            dimension_semantics=("arbitrary", "arbitrary", "arbitrary"),
            vmem_limit_bytes=VMEM_LIMIT),
        name="in_proj",
    )(h, nw, w_t, w_t, *cast_weights)
    return proj, dt_raw, casts


def _pool_kernel(u_ref, prev_ref, wg_ref, scale_ref, wup_ref, o_ref, pooled_scr, y1_scr):
    i = pl.program_id(1)
    prev_chunk = prev_ref[0]
    prev_chunk = jnp.where(i > 0, prev_chunk, jnp.zeros_like(prev_chunk))

    q_io = jax.lax.broadcasted_iota(jnp.int32, (CHUNK, 2 * CHUNK), 0)
    k_io = jax.lax.broadcasted_iota(jnp.int32, (CHUNK, 2 * CHUNK), 1)
    k_tok = _token_of(jnp.bitwise_and(k_io, CHUNK - 1)) + jnp.right_shift(k_io, 7) * CHUNK
    delta = _token_of(q_io) + CHUNK - k_tok
    tok_in_chunk = _token_of(jax.lax.broadcasted_iota(jnp.int32, (CHUNK, 1), 0))
    for g, win in enumerate(POOL_WINDOWS):
        band = jnp.where(jnp.logical_and(delta >= 0, delta < win), 1.0, 0.0).astype(bf16)
        cols = slice(g * POOL_GROUP_DIM, (g + 1) * POOL_GROUP_DIM)
        for sb in range(TM_POOL // CHUNK):
            rows = slice(sb * CHUNK, (sb + 1) * CHUNK)
            before = prev_chunk[:, cols] if sb == 0 else u_ref[0, (sb - 1) * CHUNK:sb * CHUNK, cols]
            cur = u_ref[0, rows, cols]
            wsum = _dot(band, jnp.concatenate([before, cur], axis=0))
            tok = i * TM_POOL + sb * CHUNK + tok_in_chunk - PAD_FRONT
            cnt = jnp.clip(tok + 1, 1, win).astype(f32)
            pooled_scr[rows, cols] = (wsum / cnt - cur.astype(f32)).astype(bf16)
        y1 = _dot(pooled_scr[:, cols], wg_ref[g]) * scale_ref[:, cols]
        y1_scr[:, cols] = y1.astype(bf16)
    o_ref[0] = _dot(y1_scr[...], wup_ref[...]).astype(bf16)


def _pool(layer, proj, wg, scale, wup):
    bsz, lp, _ = proj.shape
    rb = TM_POOL // CHUNK
    return pl.pallas_call(
        _pool_kernel,
        grid=(bsz, lp // TM_POOL),
        in_specs=[
            pl.BlockSpec((1, TM_POOL, D_MODEL), lambda b, i: (b, i, COL_POOL // D_MODEL)),
            pl.BlockSpec((1, CHUNK, D_MODEL),
                         lambda b, i: (b, jnp.maximum(i * rb - 1, 0), COL_POOL // D_MODEL)),
            pl.BlockSpec((POOL_GROUPS, POOL_GROUP_DIM, POOL_GROUP_DIM), lambda b, i: (0, 0, 0)),
            pl.BlockSpec((None, 1, D_MODEL), lambda b, i: (layer, 0, 0)),
            pl.BlockSpec((D_MODEL, D_MODEL), lambda b, i: (0, 0)),
        ],
        out_specs=pl.BlockSpec((1, TM_POOL, D_MODEL), lambda b, i: (b, i, 0)),
        out_shape=jax.ShapeDtypeStruct((bsz, lp, D_MODEL), bf16),
        scratch_shapes=[
            pltpu.VMEM((TM_POOL, D_MODEL), bf16),
            pltpu.VMEM((TM_POOL, D_MODEL), bf16),
        ],
        compiler_params=pltpu.CompilerParams(
            dimension_semantics=("parallel", "parallel"),
            vmem_limit_bytes=VMEM_LIMIT),
        name="pool",
    )(proj, proj, wg, scale, wup)


def _conv_silu(halo, cur, w, b, use_halo):
    n = cur.shape[1]
    tail = CONV_WIDTH - 1
    x3 = cur.astype(f32).reshape(REGS, SUBLANES, n)
    prev_tail = halo[CONV_HALO - tail * SUBLANES:, :].astype(f32)
    prev_tail = jnp.where(use_halo, prev_tail, 0.0).reshape(tail, SUBLANES, n)
    merged = jnp.concatenate([x3[REGS - tail:, :SUBLANES - 1, :], prev_tail[:, SUBLANES - 1:, :]], axis=1)
    wrapped = pltpu.roll(merged, 1, axis=1)
    acc = b.reshape(1, 1, n) + x3 * w[CONV_WIDTH - 1:CONV_WIDTH, :].reshape(1, 1, n)
    for d in range(1, CONV_WIDTH):
        shifted = jnp.concatenate([wrapped[tail - d:], x3[:REGS - d]], axis=0)
        acc = acc + shifted * w[CONV_WIDTH - 1 - d:CONV_WIDTH - d, :].reshape(1, 1, n)
    return _silu_of_twice(acc).reshape(CHUNK, n)


def _chunk_has_predecessor(k, seq_start):
    return jnp.logical_not(jnp.logical_and(seq_start, k == 0))


def _ssd_group_front(k, g, seq_start, ssd_refs):
    (xs_ref, xs_h_ref, b_ref, b_h_ref, c_ref, c_h_ref, _, _, cw_ref, cb_ref) = ssd_refs[:10]
    rows = slice(k * CHUNK, (k + 1) * CHUNK)
    use_halo = _chunk_has_predecessor(k, seq_start)

    def conv_of(cur_ref, halo_ref, cols, wcols):
        if k == 0:
            halo = halo_ref[0, :, cols]
        else:
            halo = cur_ref[0, k * CHUNK - CONV_HALO:k * CHUNK, cols]
        return _conv_silu(halo, cur_ref[0, rows, cols], cw_ref[:, wcols], cb_ref[:, wcols], use_halo)

    xcols = slice(g * GROUP_X, (g + 1) * GROUP_X)
    ncols = slice(g * D_STATE, (g + 1) * D_STATE)
    bc_b = slice(D_INNER + g * D_STATE, D_INNER + (g + 1) * D_STATE)
    bc_c = slice(D_INNER + SSM_GROUPS * D_STATE + g * D_STATE,
                 D_INNER + SSM_GROUPS * D_STATE + (g + 1) * D_STATE)
    xg = conv_of(xs_ref, xs_h_ref, xcols, xcols)
    bg = conv_of(b_ref, b_h_ref, ncols, bc_b).astype(bf16)
    cg = conv_of(c_ref, c_h_ref, ncols, bc_c)
    cb = jax.lax.dot_general(cg.astype(bf16), bg, NT_DIMS, preferred_element_type=f32)
    return xg, bg, cg, cb


def _ssd_prologue(k, seq_start, ssd_refs):
    dt_ref, dtb_ref, alog_ref, eexp_ref = ssd_refs[7], ssd_refs[10], ssd_refs[11], ssd_refs[14]
    rows = slice(k * CHUNK, (k + 1) * CHUNK)
    use_halo = _chunk_has_predecessor(k, seq_start)

    tok = _token_of(jax.lax.broadcasted_iota(jnp.int32, (1, CHUNK), 1))
    valid_t = jnp.logical_or(use_halo, tok >= PAD_FRONT)
    raw_t = (dt_ref[0, rows, :] + dtb_ref[...]).T[0:SSM_HEADS, :]
    dt_t = jnp.where(valid_t, _softplus(raw_t), 0.0)
    a_col = -jnp.exp(alog_ref[...])
    tok_r = _token_of(jax.lax.broadcasted_iota(jnp.int32, (CHUNK, CHUNK), 0))
    tok_c = _token_of(jax.lax.broadcasted_iota(jnp.int32, (CHUNK, CHUNK), 1))
    causal = tok_r >= tok_c
    tri_t = jnp.where(tok_r <= tok_c, 1.0, 0.0).astype(bf16)
    acs_t = _dot_exact_rhs(dt_t * a_col, tri_t)
    yield
    acs_last = acs_t[:, CHUNK - 1:CHUNK]
    rowarg_t = acs_t * LOG2E - jnp.log(dt_t) * LOG2E
    w_state_t = jnp.exp(acs_last - acs_t) * dt_t
    e_acs_last = jnp.exp(acs_last)

    def tokens_on_rows(x):
        pad = jnp.zeros((CHUNK - SSM_HEADS, CHUNK), f32)
        return jnp.concatenate([x, pad], axis=0).T

    acs2 = tokens_on_rows(acs_t * LOG2E)
    w_exp = _dot(tokens_on_rows(w_state_t).astype(bf16), eexp_ref[...])
    yield
    chunk_decay =tokens_on_rows(jnp.broadcast_to(e_acs_last, (SSM_HEADS, CHUNK)))[0:SUBLANES, :]
    cd_exp = _dot_exact_rhs(chunk_decay, eexp_ref[...])[0:1, :]

    return acs2, rowarg_t, w_exp, cd_exp, causal


def _ssd_group_lhs(g, front, prologue):
    _, _, cg, cb = front
    acs2, rowarg_t, _, _, causal = prologue
    lhs = []
    for r in range(HEADS_PER_GROUP):
        h = g * HEADS_PER_GROUP + r
        col = jnp.broadcast_to(acs2[:, h:h + 1], (CHUNK, CHUNK))
        lm = jnp.exp2(col - rowarg_t[h:h + 1, :])
        m = (jnp.where(causal, lm, 0.0) * cb).astype(bf16)
        ce = (cg * jnp.exp2(col)).astype(bf16)
        lhs.append(jnp.concatenate([m, ce], axis=1))
    return lhs


def _ssd_group_back(k, g, front, lhs, prologue, ssd_refs, state_scr, y_ref):
    z_ref, dskip_ref, nw_ref = ssd_refs[6], ssd_refs[12], ssd_refs[13]
    xg, bg, _, _ = front
    _, _, w_exp, cd_exp, _ = prologue
    rows = slice(k * CHUNK, (k + 1) * CHUNK)
    xcols = slice(g * GROUP_X, (g + 1) * GROUP_X)
    low_head = jax.lax.broadcasted_iota(jnp.int32, (1, LANES), 1) < SSM_HEAD_DIM
    xg_b = xg.astype(bf16)
    s_old = state_scr[g]
    s_b = s_old.astype(bf16)

    halves = []
    for half in range(HEADS_PER_GROUP // 2):
        hcols = slice(half * LANES, (half + 1) * LANES)
        rhs = jnp.concatenate([xg_b[:, hcols], s_b[:, hcols]], axis=0)
        outs = [_dot(lhs[r], rhs) for r in (2 * half, 2 * half + 1)]
        halves.append(jnp.where(low_head, outs[0], outs[1]))
    y = jnp.concatenate(halves, axis=1)

    xw = (xg * w_exp[:, xcols]).astype(bf16)
    st = jax.lax.dot_general(bg, xw, (((0,), (0,)), ((), ())), preferred_element_type=f32)
    state_scr[g] = s_old * cd_exp[:, xcols] + st

    y = y + xg * dskip_ref[:, xcols]
    y = y * _silu_of_twice(z_ref[0, rows, xcols].astype(f32))
    ms = jnp.mean(y * y, axis=-1, keepdims=True)
    y = y * jax.lax.rsqrt(ms + EPS) * nw_ref[:, xcols]
    y_ref[rows, xcols] = y.astype(bf16)


def _ssd_tile(seq_start, ssd_refs, state_scr, y_ref):
    n_chunks = TM_MLP // CHUNK
    items = [(k, g) for k in range(n_chunks) for g in range(SSM_GROUPS)]

    def advance(k):
        try:
            next(pending_prologue[k])
        except StopIteration as done:
            prologues[k] = done.value

    prologues, fronts = {}, {}
    pending_prologue = {k: _ssd_prologue(k, seq_start, ssd_refs) for k in range(n_chunks)}
    for _ in range(PROLOGUE_PARTS - 1):
        advance(0)
        yield
    advance(0)
    fronts[items[0]] = _ssd_group_front(*items[0], seq_start, ssd_refs)
    yield
    for n, (k, g) in enumerate(items):
        if n + 1 < len(items):
            fronts[items[n + 1]] = _ssd_group_front(*items[n + 1], seq_start, ssd_refs)
        if g in NEXT_PROLOGUE_AT and k + 1 < n_chunks:
            advance(k + 1)
        front = fronts.pop((k, g))
        lhs = _ssd_group_lhs(g, front, prologues[k])
        _ssd_group_back(k, g, front, lhs, prologues[k], ssd_refs, state_scr, y_ref)
        yield


def _merge_mlp_body(h_ref, gate_ref, ypool_ref, ynorm_ref, bgate_ref, wssd_ref, wo_ref,
                    mw_ref, ff1_ref, ff2_ref, fw_ref, o_ref, final):
    def dot_in_pieces(a, w_ref, row0, col0, n_cols, col_pieces):
        kw = a.shape[1] // MERGE_K_PIECES
        cw = n_cols // col_pieces
        out = []
        for p in range(col_pieces):
            acc = None
            for q in range(MERGE_K_PIECES):
                part = _dot(a[:, q * kw:(q + 1) * kw],
                            w_ref[row0 + q * kw:row0 + (q + 1) * kw, col0 + p * cw:col0 + (p + 1) * cw])
                acc = part if acc is None else acc + part
                yield
            out.append(acc)
        return jnp.concatenate(out, axis=1)

    y_ssd = yield from dot_in_pieces(ynorm_ref[...], wssd_ref, 0, 0, D_MODEL, 4)
    gates = _sigmoid(gate_ref[0].astype(f32) + bgate_ref[...])
    mix = (gates[:, :D_MODEL] * ypool_ref[0].astype(f32) + gates[:, D_MODEL:] * y_ssd).astype(bf16)
    h1 = h_ref[0] + (yield from dot_in_pieces(mix, wo_ref, 0, 0, D_MODEL, 2))
    ms = jnp.mean(h1 * h1, axis=-1, keepdims=True)
    v = (h1 * jax.lax.rsqrt(ms + EPS) * mw_ref[...]).astype(bf16)
    acc = h1
    for k in range(D_FF // FF_CHUNK):
        hid = jnp.maximum((yield from dot_in_pieces(v, ff1_ref, 0, k * FF_CHUNK, FF_CHUNK, 2)), 0.0)
        hid = (hid * hid).astype(bf16)
        acc = acc + (yield from dot_in_pieces(hid, ff2_ref, k * FF_CHUNK, 0, D_MODEL, 2))
    if final:
        ms = jnp.mean(acc * acc, axis=-1, keepdims=True)
        acc = acc * jax.lax.rsqrt(ms + EPS) * fw_ref[...]
    o_ref[0] = acc


N_SSD_REFS = 15
SCAN_STAGES_PER_MERGE_STAGE = 1
MERGE_K_PIECES = 1
PROLOGUE_PARTS = 3
NEXT_PROLOGUE_AT = (1, 3, 5)
_DONE = object()


def _mixer_mlp_kernel(*refs, final, tiles_per_row):
    ssd_refs = refs[:N_SSD_REFS]
    (h_ref, gate_ref, ypool_ref, bgate_ref, wssd_ref, wo_ref, mw_ref, ff1_ref, ff2_ref, fw_ref,
     o_ref, state_scr, ybuf) = refs[N_SSD_REFS:]
    t = pl.program_id(0)
    last = pl.num_programs(0) - 1
    seq_start = jnp.logical_and(t % tiles_per_row == 0, t < last)

    @pl.when(seq_start)
    def _():
        state_scr[...] = jnp.zeros_like(state_scr)

    @pl.when(t == 0)
    def _():
        ybuf[...] = jnp.zeros_like(ybuf)

    write_slot = jnp.bitwise_and(t, 1)
    merge = _merge_mlp_body(h_ref, gate_ref, ypool_ref, ybuf.at[1 - write_slot], bgate_ref, wssd_ref,
                            wo_ref, mw_ref, ff1_ref, ff2_ref, fw_ref, o_ref, final)
    scan = _ssd_tile(seq_start, ssd_refs, state_scr, ybuf.at[write_slot])
    pending = [scan, merge]
    while pending:
        for gen, stages in ((scan, SCAN_STAGES_PER_MERGE_STAGE), (merge, 1)):
            for _ in range(stages):
                if gen in pending and next(gen, _DONE) is _DONE:
                    pending.remove(gen)


def _mixer_mlp(layer, h, proj, dt_raw, ypool, cw, cb, dtb, alog, dskip, ssd_nw, eexp,
               bgate, wssd, wo, mw, ff1, ff2, fw):
    bsz, lp, _ = h.shape
    n_tiles = lp // TM_MLP
    n_total = bsz * n_tiles
    hb = TM_MLP // CONV_HALO

    def scan_tile(t):
        tile = jnp.minimum(t, n_total - 1)
        return tile // n_tiles, tile % n_tiles

    def merge_tile(t):
        tile = jnp.maximum(t - 1, 0)
        return tile // n_tiles, tile % n_tiles

    def cur(width, col):
        return pl.BlockSpec((1, TM_MLP, width), lambda t: (*scan_tile(t), col // width))

    def halo(width, col):
        def index(t):
            b, i = scan_tile(t)
            return b, jnp.maximum(i * hb - 1, 0), col // width
        return pl.BlockSpec((1, CONV_HALO, width), index)

    def const(shape):
        return pl.BlockSpec((None,) + shape, lambda t: (layer,) + (0,) * len(shape),
                            pipeline_mode=pl.Buffered(1))

    def resident(shape):
        return pl.BlockSpec(shape, lambda t: (0,) * len(shape), pipeline_mode=pl.Buffered(1))

    def merge_in(width, col):
        return pl.BlockSpec((1, TM_MLP, width), lambda t: (*merge_tile(t), col // width))

    n_bc = SSM_GROUPS * D_STATE
    return pl.pallas_call(
        functools.partial(_mixer_mlp_kernel, final=layer == DEPTH - 1, tiles_per_row=n_tiles),
        grid=(n_total + 1,),
        in_specs=[
            cur(D_INNER, COL_XS), halo(D_INNER, COL_XS),
            cur(n_bc, COL_B), halo(n_bc, COL_B),
            cur(n_bc, COL_C), halo(n_bc, COL_C),
            cur(D_INNER, COL_Z),
            pl.BlockSpec((1, TM_MLP, DT_LANES), lambda t: (*scan_tile(t), 0)),
            const((CONV_WIDTH, D_INNER + 2 * n_bc)),
            const((1, D_INNER + 2 * n_bc)),
            const((1, DT_LANES)),
            const((SSM_HEADS, 1)),
            const((1, D_INNER)),
            const((1, D_INNER)),
            resident((DT_LANES, D_INNER)),
            merge_in(D_MODEL, 0),
            merge_in(2 * D_MODEL, COL_GATE),
            merge_in(D_MODEL, 0),
            const((1, 2 * D_MODEL)),
            resident((D_INNER, D_MODEL)),
            resident((D_MODEL, D_MODEL)),
            const((1, D_MODEL)),
            resident((D_MODEL, D_FF)),
            resident((D_FF, D_MODEL)),
            resident((1, D_MODEL)),
        ],
        out_specs=pl.BlockSpec((1, TM_MLP, D_MODEL), lambda t: (*merge_tile(t), 0)),
        out_shape=jax.ShapeDtypeStruct((bsz, lp, D_MODEL), f32),
        scratch_shapes=[
            pltpu.VMEM((SSM_GROUPS, D_STATE, GROUP_X), f32),
            pltpu.VMEM((2, TM_MLP, D_INNER), bf16),
        ],
        compiler_params=pltpu.CompilerParams(
            dimension_semantics=("arbitrary",),
            vmem_limit_bytes=MIXER_VMEM_LIMIT),
        name="mixer_mlp",
    )(proj, proj, proj, proj, proj, proj, proj, dt_raw, cw, cb, dtb, alog, dskip, ssd_nw, eexp,
      h, proj, ypool, bgate, wssd, wo, mw, ff1, ff2, fw)


def _to_strided_rows(t):
    bsz, n, d = t.shape
    t = t.reshape(bsz, n // CHUNK, SUBLANES, REGS, d)
    return jnp.swapaxes(t, 2, 3).reshape(bsz, n, d)


def _from_strided_rows(t):
    bsz, n, d = t.shape
    t = t.reshape(bsz, n // CHUNK, REGS, SUBLANES, d)
    return jnp.swapaxes(t, 2, 3).reshape(bsz, n, d)


def kernel(x, meta_tokens, mix_norm_w, w_in, b_gate, pool_w_group, pool_scale, w_pool_up,
           conv_w, conv_b, dt_bias, a_log, d_skip, ssd_norm_w, w_ssd_out, w_o,
           mlp_norm_w, w_ff1, w_ff2, final_norm_w):
    bsz, seq, _ = x.shape
    meta = jnp.broadcast_to(meta_tokens[None].astype(x.dtype), (bsz, N_META, D_MODEL))
    first_chunk = jnp.concatenate([jnp.zeros((bsz, PAD_FRONT, D_MODEL), x.dtype), meta], axis=1)
    h = _to_strided_rows(jnp.concatenate([first_chunk, x], axis=1))

    w_t = jnp.swapaxes(w_in, 1, 2)
    row = lambda p: p[:, None, :]
    dtb = row(jnp.pad(dt_bias, ((0, 0), (0, DT_LANES - SSM_HEADS))))
    dskip = row(jnp.repeat(d_skip, SSM_HEAD_DIM, axis=-1))
    eexp = (jnp.arange(DT_LANES)[:, None] == (jnp.arange(D_INNER)[None, :] // SSM_HEAD_DIM)).astype(bf16)
    cw = conv_w * 0.5
    cb = row(conv_b * 0.5)
    alog = a_log[:, :, None]
    mix_nw, pscale, ssd_nw = row(mix_norm_w), row(pool_scale), row(ssd_norm_w)
    bgate, mlp_nw = row(b_gate), row(mlp_norm_w)
    to_cast = (w_ff1, w_ff2, w_ssd_out, w_o, w_pool_up, pool_w_group)

    for i in range(DEPTH):
        proj, dt_raw, (ff1, ff2, wssd, wo, wup, wg) = _in_proj(i, h, mix_nw, w_t, to_cast)
        ypool = _pool(i, proj, wg, pscale, wup)
        h = _mixer_mlp(i, h, proj, dt_raw, ypool, cw, cb, dtb, alog, dskip, ssd_nw, eexp,
                       bgate, wssd, wo, mlp_nw, ff1, ff2, final_norm_w[None])
    return _from_strided_rows(h)[:, h.shape[1] - seq:]
```

```python
import functools
import itertools
import math

import jax
import jax.numpy as jnp
from jax.experimental import pallas as pl
from jax.experimental.pallas import tpu as pltpu

D_MODEL = 1024
DEPTH = 4
N_META = 16
POOL_GROUPS = 4
POOL_GROUP_DIM = 256
POOL_WINDOWS = (2, 4, 8, 16)
D_INNER = 2048
SSM_HEAD_DIM = 64
SSM_HEADS = 32
SSM_GROUPS = 8
HEADS_PER_GROUP = 4
D_STATE = 128
CONV_WIDTH = 4
CHUNK = 128
D_FF = 4096
EPS = 1e-5
LOG2E = math.log2(math.e)

OFF_DT = 7168
OFF_GATE = 7200
W_BLOCK = 1024
WBLK_POOL, WBLK_Z, WBLK_XS, WBLK_B, WBLK_C = 0, 1, 3, 5, 6

COL_Z = 0
COL_XS = 2048
COL_GATE = 4096
COL_POOL = 6144
COL_B = 7168
COL_C = 8192
N_MAIN = 9216
DT_LANES = 128

PAD_FRONT = CHUNK - N_META
SUBLANES = 8
LANES = 128
REGS = CHUNK // SUBLANES
CONV_HALO = 32
GROUP_X = HEADS_PER_GROUP * SSM_HEAD_DIM

TM_PROJ = 1408
TM_POOL = 1408
TM_MLP = 384
FF_CHUNK = 1024
VMEM_LIMIT = 56 * 1024 * 1024
MIXER_VMEM_LIMIT = 60 * 1024 * 1024

f32 = jnp.float32
bf16 = jnp.bfloat16


def _dot(a, b):
    return jnp.dot(a, b, preferred_element_type=f32)


def _token_of(q):
    return jnp.right_shift(q, 3) + jnp.bitwise_and(q, SUBLANES - 1) * REGS


def _split3(x):
    p1 = x.astype(bf16)
    r1 = x - p1.astype(f32)
    p2 = r1.astype(bf16)
    p3 = (r1 - p2.astype(f32)).astype(bf16)
    return p1, p2, p3


def _dot_exact_rhs(x, e):
    p1, p2, p3 = _split3(x)
    return _dot(p1, e) + _dot(p2, e) + _dot(p3, e)


def _sigmoid(x):
    return 1.0 / (1.0 + jnp.exp(-x))


def _silu_of_twice(hx):
    return hx + hx * jnp.tanh(hx)


def _softplus(x):
    return jnp.maximum(x, 0.0) + jnp.log1p(jnp.exp(-jnp.abs(x)))


GATE_BLK0 = COL_GATE // W_BLOCK
NT_DIMS = (((1,), (1,)), ((), ()))


CAST_PLAN = (
    ((32, D_FF), D_MODEL // 32),
    ((128, D_MODEL), D_FF // 128),
    ((128, D_MODEL), D_INNER // 128),
    ((128, D_MODEL), D_MODEL // 128),
    ((128, D_MODEL), D_MODEL // 128),
    ((POOL_GROUPS, POOL_GROUP_DIM, POOL_GROUP_DIM), 1),
)
CAST_STARTS = tuple(sum(n for _, n in CAST_PLAN[:p]) for p in range(len(CAST_PLAN)))
N_CAST = len(CAST_PLAN)


def _in_proj_kernel(h_ref, nw_ref, w_ref, wdt_ref, *rest):
    cast_in = rest[:N_CAST]
    o_ref, dt_ref = rest[N_CAST:N_CAST + 2]
    cast_out = rest[N_CAST + 2:2 * N_CAST + 2]
    u_scr = rest[-1]
    j = pl.program_id(1)
    i = pl.program_id(2)
    rows = pl.ds(pl.multiple_of(i * TM_PROJ, CHUNK), TM_PROJ)

    step = (pl.program_id(0) * pl.num_programs(1) + j) * pl.num_programs(2) + i
    for src, dst, start, (_, n_blocks) in zip(cast_in, cast_out, CAST_STARTS, CAST_PLAN):
        @pl.when(jnp.logical_and(step >= start, step < start + n_blocks))
        def _(src=src, dst=dst):
            dst[...] = src[...].astype(bf16)

    @pl.when(j == 0)
    def _():
        x = h_ref[0]
        ms = jnp.mean(x * x, axis=-1, keepdims=True)
        u = x * jax.lax.rsqrt(ms + EPS) * nw_ref[...]
        r = i * TM_PROJ + jax.lax.broadcasted_iota(jnp.int32, (TM_PROJ, 1), 0)
        valid = jnp.logical_or(r >= CHUNK, jnp.bitwise_and(r, SUBLANES - 1) == SUBLANES - 1)
        u = jnp.where(valid, u, 0.0).astype(bf16)
        u_scr[rows, :] = u
        dt_ref[0] = jax.lax.dot_general(u, wdt_ref[0].astype(bf16), NT_DIMS, preferred_element_type=f32)

    w = w_ref[0].astype(bf16)
    acc = jax.lax.dot_general(u_scr[rows, :], w, NT_DIMS, preferred_element_type=f32)
    o_ref[0] = (acc * jnp.where(j < COL_XS // W_BLOCK, 0.5, 1.0)).astype(bf16)


def _w_row_of(j):
    blk = jnp.where(j < GATE_BLK0, j + WBLK_Z, jnp.where(j == COL_POOL // W_BLOCK, WBLK_POOL, j - 2))
    is_gate = jnp.logical_and(j >= GATE_BLK0, j < GATE_BLK0 + 2)
    return pl.multiple_of(jnp.where(is_gate, OFF_GATE + (j - GATE_BLK0) * W_BLOCK, blk * W_BLOCK), SUBLANES)


def _in_proj(layer, h, nw, w_t, cast_weights):
    bsz, lp, _ = h.shape
    n_i = lp // TM_PROJ
    n_j = N_MAIN // W_BLOCK
    grid = (bsz, n_j, n_i)
    assert CAST_STARTS[-1] + CAST_PLAN[-1][1] <= bsz * n_j * n_i

    def first_pass_tile(b, j, i):
        return (b, jnp.where(j == 0, i, n_i - 1), 0)

    def cast_block(start, n_blocks):
        return lambda b, j, i: jnp.clip((b * n_j + j) * n_i + i - start, 0, n_blocks - 1)

    cast_in, cast_out, cast_shapes = [], [], []
    for w, start, (blk, n_blocks) in zip(cast_weights, CAST_STARTS, CAST_PLAN):
        which = cast_block(start, n_blocks)
        tail = (0,) * (len(blk) - 1)
        cast_in.append(pl.BlockSpec((None,) + blk, lambda b, j, i, which=which, tail=tail:
                                    (layer, which(b, j, i)) + tail))
        cast_out.append(pl.BlockSpec(blk, lambda b, j, i, which=which, tail=tail: (which(b, j, i),) + tail))
        cast_shapes.append(jax.ShapeDtypeStruct(w.shape[1:], bf16))

    elem = lambda *shape: tuple(pl.Element(s) for s in shape)
    proj, dt_raw, *casts = pl.pallas_call(
        _in_proj_kernel,
        grid=grid,
        in_specs=[
            pl.BlockSpec((1, TM_PROJ, D_MODEL), first_pass_tile),
            pl.BlockSpec((None, 1, D_MODEL), lambda b, j, i: (layer, 0, 0)),
            pl.BlockSpec(elem(1, W_BLOCK, D_MODEL), lambda b, j, i: (layer, _w_row_of(j), 0)),
            pl.BlockSpec(elem(1, DT_LANES, D_MODEL), lambda b, j, i: (layer, OFF_DT, 0)),
        ] + cast_in,
        out_specs=[
            pl.BlockSpec((1, TM_PROJ, W_BLOCK), lambda b, j, i: (b, i, j)),
            pl.BlockSpec((1, TM_PROJ, DT_LANES), first_pass_tile),
        ] + cast_out,
        out_shape=[
            jax.ShapeDtypeStruct((bsz, lp, N_MAIN), bf16),
            jax.ShapeDtypeStruct((bsz, lp, DT_LANES), f32),
        ] + cast_shapes,
        scratch_shapes=[pltpu.VMEM((lp, D_MODEL), bf16)],
        compiler_params=pltpu.CompilerParams(
            dimension_semantics=("arbitrary", "arbitrary", "arbitrary"),
            vmem_limit_bytes=VMEM_LIMIT),
        name="in_proj",
    )(h, nw, w_t, w_t, *cast_weights)
    return proj, dt_raw, casts


def _pool_kernel(u_ref, prev_ref, wg_ref, scale_ref, wup_ref, o_ref, pooled_scr, y1_scr):
    i = pl.program_id(1)
    prev_chunk = prev_ref[0]
    prev_chunk = jnp.where(i > 0, prev_chunk, jnp.zeros_like(prev_chunk))

    q_io = jax.lax.broadcasted_iota(jnp.int32, (CHUNK, 2 * CHUNK), 0)
    k_io = jax.lax.broadcasted_iota(jnp.int32, (CHUNK, 2 * CHUNK), 1)
    k_tok = _token_of(jnp.bitwise_and(k_io, CHUNK - 1)) + jnp.right_shift(k_io, 7) * CHUNK
    delta = _token_of(q_io) + CHUNK - k_tok
    tok_in_chunk = _token_of(jax.lax.broadcasted_iota(jnp.int32, (CHUNK, 1), 0))
    for g, win in enumerate(POOL_WINDOWS):
        band = jnp.where(jnp.logical_and(delta >= 0, delta < win), 1.0, 0.0).astype(bf16)
        cols = slice(g * POOL_GROUP_DIM, (g + 1) * POOL_GROUP_DIM)
        for sb in range(TM_POOL // CHUNK):
            rows = slice(sb * CHUNK, (sb + 1) * CHUNK)
            before = prev_chunk[:, cols] if sb == 0 else u_ref[0, (sb - 1) * CHUNK:sb * CHUNK, cols]
            cur = u_ref[0, rows, cols]
            wsum = _dot(band, jnp.concatenate([before, cur], axis=0))
            tok = i * TM_POOL + sb * CHUNK + tok_in_chunk - PAD_FRONT
            cnt = jnp.clip(tok + 1, 1, win).astype(f32)
            pooled_scr[rows, cols] = (wsum / cnt - cur.astype(f32)).astype(bf16)
        y1 = _dot(pooled_scr[:, cols], wg_ref[g]) * scale_ref[:, cols]
        y1_scr[:, cols] = y1.astype(bf16)
    o_ref[0] = _dot(y1_scr[...], wup_ref[...]).astype(bf16)


def _pool(layer, proj, wg, scale, wup):
    bsz, lp, _ = proj.shape
    rb = TM_POOL // CHUNK
    return pl.pallas_call(
        _pool_kernel,
        grid=(bsz, lp // TM_POOL),
        in_specs=[
            pl.BlockSpec((1, TM_POOL, D_MODEL), lambda b, i: (b, i, COL_POOL // D_MODEL)),
            pl.BlockSpec((1, CHUNK, D_MODEL),
                         lambda b, i: (b, jnp.maximum(i * rb - 1, 0), COL_POOL // D_MODEL)),
            pl.BlockSpec((POOL_GROUPS, POOL_GROUP_DIM, POOL_GROUP_DIM), lambda b, i: (0, 0, 0)),
            pl.BlockSpec((None, 1, D_MODEL), lambda b, i: (layer, 0, 0)),
            pl.BlockSpec((D_MODEL, D_MODEL), lambda b, i: (0, 0)),
        ],
        out_specs=pl.BlockSpec((1, TM_POOL, D_MODEL), lambda b, i: (b, i, 0)),
        out_shape=jax.ShapeDtypeStruct((bsz, lp, D_MODEL), bf16),
        scratch_shapes=[
            pltpu.VMEM((TM_POOL, D_MODEL), bf16),
            pltpu.VMEM((TM_POOL, D_MODEL), bf16),
        ],
        compiler_params=pltpu.CompilerParams(
            dimension_semantics=("parallel", "parallel"),
            vmem_limit_bytes=VMEM_LIMIT),
        name="pool",
    )(proj, proj, wg, scale, wup)


def _conv_silu(halo, cur, w, b, use_halo):
    n = cur.shape[1]
    tail = CONV_WIDTH - 1
    x3 = cur.astype(f32).reshape(REGS, SUBLANES, n)
    prev_tail = halo[CONV_HALO - tail * SUBLANES:, :].astype(f32)
    prev_tail = jnp.where(use_halo, prev_tail, 0.0).reshape(tail, SUBLANES, n)
    merged = jnp.concatenate([x3[REGS - tail:, :SUBLANES - 1, :], prev_tail[:, SUBLANES - 1:, :]], axis=1)
    wrapped = pltpu.roll(merged, 1, axis=1)
    acc = b.reshape(1, 1, n) + x3 * w[CONV_WIDTH - 1:CONV_WIDTH, :].reshape(1, 1, n)
    for d in range(1, CONV_WIDTH):
        shifted = jnp.concatenate([wrapped[tail - d:], x3[:REGS - d]], axis=0)
        acc = acc + shifted * w[CONV_WIDTH - 1 - d:CONV_WIDTH - d, :].reshape(1, 1, n)
    return _silu_of_twice(acc).reshape(CHUNK, n)


def _chunk_has_predecessor(k, seq_start):
    return jnp.logical_not(jnp.logical_and(seq_start, k == 0))


def _ssd_group_front(k, g, seq_start, ssd_refs):
    (xs_ref, xs_h_ref, b_ref, b_h_ref, c_ref, c_h_ref, _, _, cw_ref, cb_ref) = ssd_refs[:10]
    rows = slice(k * CHUNK, (k + 1) * CHUNK)
    use_halo = _chunk_has_predecessor(k, seq_start)

    def conv_of(cur_ref, halo_ref, cols, wcols):
        if k == 0:
            halo = halo_ref[0, :, cols]
        else:
            halo = cur_ref[0, k * CHUNK - CONV_HALO:k * CHUNK, cols]
        return _conv_silu(halo, cur_ref[0, rows, cols], cw_ref[:, wcols], cb_ref[:, wcols], use_halo)

    xcols = slice(g * GROUP_X, (g + 1) * GROUP_X)
    ncols = slice(g * D_STATE, (g + 1) * D_STATE)
    bc_b = slice(D_INNER + g * D_STATE, D_INNER + (g + 1) * D_STATE)
    bc_c = slice(D_INNER + SSM_GROUPS * D_STATE + g * D_STATE,
                 D_INNER + SSM_GROUPS * D_STATE + (g + 1) * D_STATE)
    xg = conv_of(xs_ref, xs_h_ref, xcols, xcols)
    bg = conv_of(b_ref, b_h_ref, ncols, bc_b).astype(bf16)
    cg = conv_of(c_ref, c_h_ref, ncols, bc_c)
    cb = jax.lax.dot_general(cg.astype(bf16), bg, NT_DIMS, preferred_element_type=f32)
    return xg, bg, cg, cb


def _ssd_prologue(k, seq_start, ssd_refs):
    dt_ref, dtb_ref, alog_ref, eexp_ref = ssd_refs[7], ssd_refs[10], ssd_refs[11], ssd_refs[14]
    rows = slice(k * CHUNK, (k + 1) * CHUNK)
    use_halo = _chunk_has_predecessor(k, seq_start)

    tok = _token_of(jax.lax.broadcasted_iota(jnp.int32, (1, CHUNK), 1))
    valid_t = jnp.logical_or(use_halo, tok >= PAD_FRONT)
    raw_t = (dt_ref[0, rows, :] + dtb_ref[...]).T[0:SSM_HEADS, :]
    dt_t = jnp.where(valid_t, _softplus(raw_t), 0.0)
    a_col = -jnp.exp(alog_ref[...])
    tok_r = _token_of(jax.lax.broadcasted_iota(jnp.int32, (CHUNK, CHUNK), 0))
    tok_c = _token_of(jax.lax.broadcasted_iota(jnp.int32, (CHUNK, CHUNK), 1))
    causal = tok_r >= tok_c
    tri_t = jnp.where(tok_r <= tok_c, 1.0, 0.0).astype(bf16)
    acs_t = _dot_exact_rhs(dt_t * a_col, tri_t)
    yield
    acs_last = acs_t[:, CHUNK - 1:CHUNK]
    rowarg_t = acs_t * LOG2E - jnp.log(dt_t) * LOG2E
    w_state_t = jnp.exp(acs_last - acs_t) * dt_t
    e_acs_last = jnp.exp(acs_last)

    def tokens_on_rows(x):
        pad = jnp.zeros((CHUNK - SSM_HEADS, CHUNK), f32)
        return jnp.concatenate([x, pad], axis=0).T

    acs2 = tokens_on_rows(acs_t * LOG2E)
    w_exp = _dot(tokens_on_rows(w_state_t).astype(bf16), eexp_ref[...])
    yield
    chunk_decay =tokens_on_rows(jnp.broadcast_to(e_acs_last, (SSM_HEADS, CHUNK)))[0:SUBLANES, :]
    cd_exp = _dot_exact_rhs(chunk_decay, eexp_ref[...])[0:1, :]

    return acs2, rowarg_t, w_exp, cd_exp, causal


def _ssd_group_lhs(g, front, prologue):
    _, _, cg, cb = front
    acs2, rowarg_t, _, _, causal = prologue
    lhs = []
    for r in range(HEADS_PER_GROUP):
        h = g * HEADS_PER_GROUP + r
        col = jnp.broadcast_to(acs2[:, h:h + 1], (CHUNK, CHUNK))
        lm = jnp.exp2(col - rowarg_t[h:h + 1, :])
        m = (jnp.where(causal, lm, 0.0) * cb).astype(bf16)
        ce = (cg * jnp.exp2(col)).astype(bf16)
        lhs.append(jnp.concatenate([m, ce], axis=1))
    return lhs


def _ssd_group_back(k, g, front, lhs, prologue, ssd_refs, state_scr, y_ref):
    z_ref, dskip_ref, nw_ref = ssd_refs[6], ssd_refs[12], ssd_refs[13]
    xg, bg, _, _ = front
    _, _, w_exp, cd_exp, _ = prologue
    rows = slice(k * CHUNK, (k + 1) * CHUNK)
    xcols = slice(g * GROUP_X, (g + 1) * GROUP_X)
    low_head = jax.lax.broadcasted_iota(jnp.int32, (1, LANES), 1) < SSM_HEAD_DIM
    xg_b = xg.astype(bf16)
    s_old = state_scr[g]
    s_b = s_old.astype(bf16)

    halves = []
    for half in range(HEADS_PER_GROUP // 2):
        hcols = slice(half * LANES, (half + 1) * LANES)
        rhs = jnp.concatenate([xg_b[:, hcols], s_b[:, hcols]], axis=0)
        outs = [_dot(lhs[r], rhs) for r in (2 * half, 2 * half + 1)]
        halves.append(jnp.where(low_head, outs[0], outs[1]))
    y = jnp.concatenate(halves, axis=1)

    xw = (xg * w_exp[:, xcols]).astype(bf16)
    st = jax.lax.dot_general(bg, xw, (((0,), (0,)), ((), ())), preferred_element_type=f32)
    state_scr[g] = s_old * cd_exp[:, xcols] + st

    y = y + xg * dskip_ref[:, xcols]
    y = y * _silu_of_twice(z_ref[0, rows, xcols].astype(f32))
    ms = jnp.mean(y * y, axis=-1, keepdims=True)
    y = y * jax.lax.rsqrt(ms + EPS) * nw_ref[:, xcols]
    y_ref[rows, xcols] = y.astype(bf16)


def _ssd_tile(seq_start, ssd_refs, state_scr, y_ref):
    n_chunks = TM_MLP // CHUNK
    items = [(k, g) for k in range(n_chunks) for g in range(SSM_GROUPS)]

    def advance(k):
        try:
            next(pending_prologue[k])
        except StopIteration as done:
            prologues[k] = done.value

    prologues, fronts = {}, {}
    pending_prologue = {k: _ssd_prologue(k, seq_start, ssd_refs) for k in range(n_chunks)}
    for _ in range(PROLOGUE_PARTS - 1):
        advance(0)
        yield
    advance(0)
    fronts[items[0]] = _ssd_group_front(*items[0], seq_start, ssd_refs)
    yield
    for n, (k, g) in enumerate(items):
        if n + 1 < len(items):
            fronts[items[n + 1]] = _ssd_group_front(*items[n + 1], seq_start, ssd_refs)
        if g in NEXT_PROLOGUE_AT and k + 1 < n_chunks:
            advance(k + 1)
        front = fronts.pop((k, g))
        lhs = _ssd_group_lhs(g, front, prologues[k])
        _ssd_group_back(k, g, front, lhs, prologues[k], ssd_refs, state_scr, y_ref)
        yield


def _merge_mlp_body(h_ref, gate_ref, ypool_ref, ynorm_ref, bgate_ref, wssd_ref, wo_ref,
                    mw_ref, ff1_ref, ff2_ref, fw_ref, o_ref, final):
    def dot_in_pieces(a, w_ref, row0, col0, n_cols, col_pieces):
        kw = a.shape[1] // MERGE_K_PIECES
        cw = n_cols // col_pieces
        out = []
        for p in range(col_pieces):
            acc = None
            for q in range(MERGE_K_PIECES):
                part = _dot(a[:, q * kw:(q + 1) * kw],
                            w_ref[row0 + q * kw:row0 + (q + 1) * kw, col0 + p * cw:col0 + (p + 1) * cw])
                acc = part if acc is None else acc + part
                yield
            out.append(acc)
        return jnp.concatenate(out, axis=1)

    y_ssd = yield from dot_in_pieces(ynorm_ref[...], wssd_ref, 0, 0, D_MODEL, 4)
    gates = _sigmoid(gate_ref[0].astype(f32) + bgate_ref[...])
    mix = (gates[:, :D_MODEL] * ypool_ref[0].astype(f32) + gates[:, D_MODEL:] * y_ssd).astype(bf16)
    h1 = h_ref[0] + (yield from dot_in_pieces(mix, wo_ref, 0, 0, D_MODEL, 2))
    ms = jnp.mean(h1 * h1, axis=-1, keepdims=True)
    v = (h1 * jax.lax.rsqrt(ms + EPS) * mw_ref[...]).astype(bf16)
    acc = h1
    for k in range(D_FF // FF_CHUNK):
        hid = jnp.maximum((yield from dot_in_pieces(v, ff1_ref, 0, k * FF_CHUNK, FF_CHUNK, 2)), 0.0)
        hid = (hid * hid).astype(bf16)
        acc = acc + (yield from dot_in_pieces(hid, ff2_ref, k * FF_CHUNK, 0, D_MODEL, 2))
    if final:
        ms = jnp.mean(acc * acc, axis=-1, keepdims=True)
        acc = acc * jax.lax.rsqrt(ms + EPS) * fw_ref[...]
    o_ref[0] = acc


N_SSD_REFS = 15
SCAN_STAGES_PER_MERGE_STAGE = 1
MERGE_K_PIECES = 1
PROLOGUE_PARTS = 3
NEXT_PROLOGUE_AT = (0, 2, 4)
_DONE = object()


def _mixer_mlp_kernel(*refs, final, tiles_per_row):
    ssd_refs = refs[:N_SSD_REFS]
    (h_ref, gate_ref, ypool_ref, bgate_ref, wssd_ref, wo_ref, mw_ref, ff1_ref, ff2_ref, fw_ref,
     o_ref, state_scr, ybuf) = refs[N_SSD_REFS:]
    t = pl.program_id(0)
    last = pl.num_programs(0) - 1
    seq_start = jnp.logical_and(t % tiles_per_row == 0, t < last)

    @pl.when(seq_start)
    def _():
        state_scr[...] = jnp.zeros_like(state_scr)

    @pl.when(t == 0)
    def _():
        ybuf[...] = jnp.zeros_like(ybuf)

    write_slot = jnp.bitwise_and(t, 1)
    merge = _merge_mlp_body(h_ref, gate_ref, ypool_ref, ybuf.at[1 - write_slot], bgate_ref, wssd_ref,
                            wo_ref, mw_ref, ff1_ref, ff2_ref, fw_ref, o_ref, final)
    scan = _ssd_tile(seq_start, ssd_refs, state_scr, ybuf.at[write_slot])
    pending = [scan, merge]
    while pending:
        for gen, stages in ((scan, SCAN_STAGES_PER_MERGE_STAGE), (merge, 1)):
            for _ in range(stages):
                if gen in pending and next(gen, _DONE) is _DONE:
                    pending.remove(gen)


def _mixer_mlp(layer, h, proj, dt_raw, ypool, cw, cb, dtb, alog, dskip, ssd_nw, eexp,
               bgate, wssd, wo, mw, ff1, ff2, fw):
    bsz, lp, _ = h.shape
    n_tiles = lp // TM_MLP
    n_total = bsz * n_tiles
    hb = TM_MLP // CONV_HALO

    def scan_tile(t):
        tile = jnp.minimum(t, n_total - 1)
        return tile // n_tiles, tile % n_tiles

    def merge_tile(t):
        tile = jnp.maximum(t - 1, 0)
        return tile // n_tiles, tile % n_tiles

    def cur(width, col):
        return pl.BlockSpec((1, TM_MLP, width), lambda t: (*scan_tile(t), col // width))

    def halo(width, col):
        def index(t):
            b, i = scan_tile(t)
            return b, jnp.maximum(i * hb - 1, 0), col // width
        return pl.BlockSpec((1, CONV_HALO, width), index)

    def const(shape):
        return pl.BlockSpec((None,) + shape, lambda t: (layer,) + (0,) * len(shape),
                            pipeline_mode=pl.Buffered(1))

    def resident(shape):
        return pl.BlockSpec(shape, lambda t: (0,) * len(shape), pipeline_mode=pl.Buffered(1))

    def merge_in(width, col):
        return pl.BlockSpec((1, TM_MLP, width), lambda t: (*merge_tile(t), col // width))

    n_bc = SSM_GROUPS * D_STATE
    return pl.pallas_call(
        functools.partial(_mixer_mlp_kernel, final=layer == DEPTH - 1, tiles_per_row=n_tiles),
        grid=(n_total + 1,),
        in_specs=[
            cur(D_INNER, COL_XS), halo(D_INNER, COL_XS),
            cur(n_bc, COL_B), halo(n_bc, COL_B),
            cur(n_bc, COL_C), halo(n_bc, COL_C),
            cur(D_INNER, COL_Z),
            pl.BlockSpec((1, TM_MLP, DT_LANES), lambda t: (*scan_tile(t), 0)),
            const((CONV_WIDTH, D_INNER + 2 * n_bc)),
            const((1, D_INNER + 2 * n_bc)),
            const((1, DT_LANES)),
            const((SSM_HEADS, 1)),
            const((1, D_INNER)),
            const((1, D_INNER)),
            resident((DT_LANES, D_INNER)),
            merge_in(D_MODEL, 0),
            merge_in(2 * D_MODEL, COL_GATE),
            merge_in(D_MODEL, 0),
            const((1, 2 * D_MODEL)),
            resident((D_INNER, D_MODEL)),
            resident((D_MODEL, D_MODEL)),
            const((1, D_MODEL)),
            resident((D_MODEL, D_FF)),
            resident((D_FF, D_MODEL)),
            resident((1, D_MODEL)),
        ],
        out_specs=pl.BlockSpec((1, TM_MLP, D_MODEL), lambda t: (*merge_tile(t), 0)),
        out_shape=jax.ShapeDtypeStruct((bsz, lp, D_MODEL), f32),
        scratch_shapes=[
            pltpu.VMEM((SSM_GROUPS, D_STATE, GROUP_X), f32),
            pltpu.VMEM((2, TM_MLP, D_INNER), bf16),
        ],
        compiler_params=pltpu.CompilerParams(
            dimension_semantics=("arbitrary",),
            vmem_limit_bytes=MIXER_VMEM_LIMIT),
        name="mixer_mlp",
    )(proj, proj, proj, proj, proj, proj, proj, dt_raw, cw, cb, dtb, alog, dskip, ssd_nw, eexp,
      h, proj, ypool, bgate, wssd, wo, mw, ff1, ff2, fw)


def _to_strided_rows(t):
    bsz, n, d = t.shape
    t = t.reshape(bsz, n // CHUNK, SUBLANES, REGS, d)
    return jnp.swapaxes(t, 2, 3).reshape(bsz, n, d)


def _from_strided_rows(t):
    bsz, n, d = t.shape
    t = t.reshape(bsz, n // CHUNK, REGS, SUBLANES, d)
    return jnp.swapaxes(t, 2, 3).reshape(bsz, n, d)


def kernel(x, meta_tokens, mix_norm_w, w_in, b_gate, pool_w_group, pool_scale, w_pool_up,
           conv_w, conv_b, dt_bias, a_log, d_skip, ssd_norm_w, w_ssd_out, w_o,
           mlp_norm_w, w_ff1, w_ff2, final_norm_w):
    bsz, seq, _ = x.shape
    meta = jnp.broadcast_to(meta_tokens[None].astype(x.dtype), (bsz, N_META, D_MODEL))
    first_chunk = jnp.concatenate([jnp.zeros((bsz, PAD_FRONT, D_MODEL), x.dtype), meta], axis=1)
    h = _to_strided_rows(jnp.concatenate([first_chunk, x], axis=1))

    w_t = jnp.swapaxes(w_in, 1, 2)
    row = lambda p: p[:, None, :]
    dtb = row(jnp.pad(dt_bias, ((0, 0), (0, DT_LANES - SSM_HEADS))))
    dskip = row(jnp.repeat(d_skip, SSM_HEAD_DIM, axis=-1))
    eexp = (jnp.arange(DT_LANES)[:, None] == (jnp.arange(D_INNER)[None, :] // SSM_HEAD_DIM)).astype(bf16)
    cw = conv_w * 0.5
    cb = row(conv_b * 0.5)
    alog = a_log[:, :, None]
    mix_nw, pscale, ssd_nw = row(mix_norm_w), row(pool_scale), row(ssd_norm_w)
    bgate, mlp_nw = row(b_gate), row(mlp_norm_w)
    to_cast = (w_ff1, w_ff2, w_ssd_out, w_o, w_pool_up, pool_w_group)

    for i in range(DEPTH):
        proj, dt_raw, (ff1, ff2, wssd, wo, wup, wg) = _in_proj(i, h, mix_nw, w_t, to_cast)
        ypool = _pool(i, proj, wg, pscale, wup)
        h = _mixer_mlp(i, h, proj, dt_raw, ypool, cw, cb, dtb, alog, dskip, ssd_nw, eexp,
                       bgate, wssd, wo, mlp_nw, ff1, ff2, final_norm_w[None])
    return _from_strided_rows(h)[:, h.shape[1] - seq:]
```

```python
import functools
import itertools
import math

import jax
import jax.numpy as jnp
from jax.experimental import pallas as pl
from jax.experimental.pallas import tpu as pltpu

D_MODEL = 1024
DEPTH = 4
N_META = 16
POOL_GROUPS = 4
POOL_GROUP_DIM = 256
POOL_WINDOWS = (2, 4, 8, 16)
D_INNER = 2048
SSM_HEAD_DIM = 64
SSM_HEADS = 32
SSM_GROUPS = 8
HEADS_PER_GROUP = 4
D_STATE = 128
CONV_WIDTH = 4
CHUNK = 128
D_FF = 4096
EPS = 1e-5
LOG2E = math.log2(math.e)

OFF_DT = 7168
OFF_GATE = 7200
W_BLOCK = 1024
WBLK_POOL, WBLK_Z, WBLK_XS, WBLK_B, WBLK_C = 0, 1, 3, 5, 6

COL_Z = 0
COL_XS = 2048
COL_GATE = 4096
COL_POOL = 6144
COL_B = 7168
COL_C = 8192
N_MAIN = 9216
DT_LANES = 128

PAD_FRONT = CHUNK - N_META
SUBLANES = 8
LANES = 128
REGS = CHUNK // SUBLANES
CONV_HALO = 32
GROUP_X = HEADS_PER_GROUP * SSM_HEAD_DIM

TM_PROJ = 1408
TM_POOL = 1408
TM_MLP = 384
FF_CHUNK = 1024
VMEM_LIMIT = 56 * 1024 * 1024
MIXER_VMEM_LIMIT = 60 * 1024 * 1024

f32 = jnp.float32
bf16 = jnp.bfloat16


def _dot(a, b):
    return jnp.dot(a, b, preferred_element_type=f32)


def _token_of(q):
    return jnp.right_shift(q, 3) + jnp.bitwise_and(q, SUBLANES - 1) * REGS


def _split3(x):
    p1 = x.astype(bf16)
    r1 = x - p1.astype(f32)
    p2 = r1.astype(bf16)
    p3 = (r1 - p2.astype(f32)).astype(bf16)
    return p1, p2, p3


def _dot_exact_rhs(x, e):
    p1, p2, p3 = _split3(x)
    return _dot(p1, e) + _dot(p2, e) + _dot(p3, e)


def _sigmoid(x):
    return 1.0 / (1.0 + jnp.exp(-x))


def _silu_of_twice(hx):
    return hx + hx * jnp.tanh(hx)


def _softplus(x):
    return jnp.maximum(x, 0.0) + jnp.log1p(jnp.exp(-jnp.abs(x)))


GATE_BLK0 = COL_GATE // W_BLOCK
NT_DIMS = (((1,), (1,)), ((), ()))


CAST_PLAN = (
    ((32, D_FF), D_MODEL // 32),
    ((128, D_MODEL), D_FF // 128),
    ((128, D_MODEL), D_INNER // 128),
    ((128, D_MODEL), D_MODEL // 128),
    ((128, D_MODEL), D_MODEL // 128),
    ((POOL_GROUPS, POOL_GROUP_DIM, POOL_GROUP_DIM), 1),
)
CAST_STARTS = tuple(sum(n for _, n in CAST_PLAN[:p]) for p in range(len(CAST_PLAN)))
N_CAST = len(CAST_PLAN)


def _in_proj_kernel(h_ref, nw_ref, w_ref, wdt_ref, *rest):
    cast_in = rest[:N_CAST]
    o_ref, dt_ref = rest[N_CAST:N_CAST + 2]
    cast_out = rest[N_CAST + 2:2 * N_CAST + 2]
    u_scr = rest[-1]
    j = pl.program_id(1)
    i = pl.program_id(2)
    rows = pl.ds(pl.multiple_of(i * TM_PROJ, CHUNK), TM_PROJ)

    step = (pl.program_id(0) * pl.num_programs(1) + j) * pl.num_programs(2) + i
    for src, dst, start, (_, n_blocks) in zip(cast_in, cast_out, CAST_STARTS, CAST_PLAN):
        @pl.when(jnp.logical_and(step >= start, step < start + n_blocks))
        def _(src=src, dst=dst):
            dst[...] = src[...].astype(bf16)

    @pl.when(j == 0)
    def _():
        x = h_ref[0]
        ms = jnp.mean(x * x, axis=-1, keepdims=True)
        u = x * jax.lax.rsqrt(ms + EPS) * nw_ref[...]
        r = i * TM_PROJ + jax.lax.broadcasted_iota(jnp.int32, (TM_PROJ, 1), 0)
        valid = jnp.logical_or(r >= CHUNK, jnp.bitwise_and(r, SUBLANES - 1) == SUBLANES - 1)
        u = jnp.where(valid, u, 0.0).astype(bf16)
        u_scr[rows, :] = u
        dt_ref[0] = jax.lax.dot_general(u, wdt_ref[0].astype(bf16), NT_DIMS, preferred_element_type=f32)

    w = w_ref[0].astype(bf16)
    acc = jax.lax.dot_general(u_scr[rows, :], w, NT_DIMS, preferred_element_type=f32)
    o_ref[0] = (acc * jnp.where(j < COL_XS // W_BLOCK, 0.5, 1.0)).astype(bf16)


def _w_row_of(j):
    blk = jnp.where(j < GATE_BLK0, j + WBLK_Z, jnp.where(j == COL_POOL // W_BLOCK, WBLK_POOL, j - 2))
    is_gate = jnp.logical_and(j >= GATE_BLK0, j < GATE_BLK0 + 2)
    return pl.multiple_of(jnp.where(is_gate, OFF_GATE + (j - GATE_BLK0) * W_BLOCK, blk * W_BLOCK), SUBLANES)


def _in_proj(layer, h, nw, w_t, cast_weights):
    bsz, lp, _ = h.shape
    n_i = lp // TM_PROJ
    n_j = N_MAIN // W_BLOCK
    grid = (bsz, n_j, n_i)
    assert CAST_STARTS[-1] + CAST_PLAN[-1][1] <= bsz * n_j * n_i

    def first_pass_tile(b, j, i):
        return (b, jnp.where(j == 0, i, n_i - 1), 0)

    def cast_block(start, n_blocks):
        return lambda b, j, i: jnp.clip((b * n_j + j) * n_i + i - start, 0, n_blocks - 1)

    cast_in, cast_out, cast_shapes = [], [], []
    for w, start, (blk, n_blocks) in zip(cast_weights, CAST_STARTS, CAST_PLAN):
        which = cast_block(start, n_blocks)
        tail = (0,) * (len(blk) - 1)
        cast_in.append(pl.BlockSpec((None,) + blk, lambda b, j, i, which=which, tail=tail:
                                    (layer, which(b, j, i)) + tail))
        cast_out.append(pl.BlockSpec(blk, lambda b, j, i, which=which, tail=tail: (which(b, j, i),) + tail))
        cast_shapes.append(jax.ShapeDtypeStruct(w.shape[1:], bf16))

    elem = lambda *shape: tuple(pl.Element(s) for s in shape)
    proj, dt_raw, *casts = pl.pallas_call(
        _in_proj_kernel,
        grid=grid,
        in_specs=[
            pl.BlockSpec((1, TM_PROJ, D_MODEL), first_pass_tile),
            pl.BlockSpec((None, 1, D_MODEL), lambda b, j, i: (layer, 0, 0)),
            pl.BlockSpec(elem(1, W_BLOCK, D_MODEL), lambda b, j, i: (layer, _w_row_of(j), 0)),
            pl.BlockSpec(elem(1, DT_LANES, D_MODEL), lambda b, j, i: (layer, OFF_DT, 0)),
        ] + cast_in,
        out_specs=[
            pl.BlockSpec((1, TM_PROJ, W_BLOCK), lambda b, j, i: (b, i, j)),
            pl.BlockSpec((1, TM_PROJ, DT_LANES), first_pass_tile),
        ] + cast_out,
        out_shape=[
            jax.ShapeDtypeStruct((bsz, lp, N_MAIN), bf16),
            jax.ShapeDtypeStruct((bsz, lp, DT_LANES), f32),
        ] + cast_shapes,
        scratch_shapes=[pltpu.VMEM((lp, D_MODEL), bf16)],
        compiler_params=pltpu.CompilerParams(
            dimension_semantics=("arbitrary", "arbitrary", "arbitrary"),
            vmem_limit_bytes=VMEM_LIMIT),
        name="in_proj",
    )(h, nw, w_t, w_t, *cast_weights)
    return proj, dt_raw, casts


def _pool_kernel(u_ref, prev_ref, wg_ref, scale_ref, wup_ref, o_ref, pooled_scr, y1_scr):
    i = pl.program_id(1)
    prev_chunk = prev_ref[0]
    prev_chunk = jnp.where(i > 0, prev_chunk, jnp.zeros_like(prev_chunk))

    q_io = jax.lax.broadcasted_iota(jnp.int32, (CHUNK, 2 * CHUNK), 0)
    k_io = jax.lax.broadcasted_iota(jnp.int32, (CHUNK, 2 * CHUNK), 1)
    k_tok = _token_of(jnp.bitwise_and(k_io, CHUNK - 1)) + jnp.right_shift(k_io, 7) * CHUNK
    delta = _token_of(q_io) + CHUNK - k_tok
    tok_in_chunk = _token_of(jax.lax.broadcasted_iota(jnp.int32, (CHUNK, 1), 0))
    for g, win in enumerate(POOL_WINDOWS):
        band = jnp.where(jnp.logical_and(delta >= 0, delta < win), 1.0, 0.0).astype(bf16)
        cols = slice(g * POOL_GROUP_DIM, (g + 1) * POOL_GROUP_DIM)
        for sb in range(TM_POOL // CHUNK):
            rows = slice(sb * CHUNK, (sb + 1) * CHUNK)
            before = prev_chunk[:, cols] if sb == 0 else u_ref[0, (sb - 1) * CHUNK:sb * CHUNK, cols]
            cur = u_ref[0, rows, cols]
            wsum = _dot(band, jnp.concatenate([before, cur], axis=0))
            tok = i * TM_POOL + sb * CHUNK + tok_in_chunk - PAD_FRONT
            cnt = jnp.clip(tok + 1, 1, win).astype(f32)
            pooled_scr[rows, cols] = (wsum / cnt - cur.astype(f32)).astype(bf16)
        y1 = _dot(pooled_scr[:, cols], wg_ref[g]) * scale_ref[:, cols]
        y1_scr[:, cols] = y1.astype(bf16)
    o_ref[0] = _dot(y1_scr[...], wup_ref[...]).astype(bf16)


def _pool(layer, proj, wg, scale, wup):
    bsz, lp, _ = proj.shape
    rb = TM_POOL // CHUNK
    return pl.pallas_call(
        _pool_kernel,
        grid=(bsz, lp // TM_POOL),
        in_specs=[
            pl.BlockSpec((1, TM_POOL, D_MODEL), lambda b, i: (b, i, COL_POOL // D_MODEL)),
            pl.BlockSpec((1, CHUNK, D_MODEL),
                         lambda b, i: (b, jnp.maximum(i * rb - 1, 0), COL_POOL // D_MODEL)),
            pl.BlockSpec((POOL_GROUPS, POOL_GROUP_DIM, POOL_GROUP_DIM), lambda b, i: (0, 0, 0)),
            pl.BlockSpec((None, 1, D_MODEL), lambda b, i: (layer, 0, 0)),
            pl.BlockSpec((D_MODEL, D_MODEL), lambda b, i: (0, 0)),
        ],
        out_specs=pl.BlockSpec((1, TM_POOL, D_MODEL), lambda b, i: (b, i, 0)),
        out_shape=jax.ShapeDtypeStruct((bsz, lp, D_MODEL), bf16),
        scratch_shapes=[
            pltpu.VMEM((TM_POOL, D_MODEL), bf16),
            pltpu.VMEM((TM_POOL, D_MODEL), bf16),
        ],
        compiler_params=pltpu.CompilerParams(
            dimension_semantics=("parallel", "parallel"),
            vmem_limit_bytes=VMEM_LIMIT),
        name="pool",
    )(proj, proj, wg, scale, wup)


def _conv_silu(halo, cur, w, b, use_halo):
    n = cur.shape[1]
    tail = CONV_WIDTH - 1
    x3 = cur.astype(f32).reshape(REGS, SUBLANES, n)
    prev_tail = halo[CONV_HALO - tail * SUBLANES:, :].astype(f32)
    prev_tail = jnp.where(use_halo, prev_tail, 0.0).reshape(tail, SUBLANES, n)
    merged = jnp.concatenate([x3[REGS - tail:, :SUBLANES - 1, :], prev_tail[:, SUBLANES - 1:, :]], axis=1)
    wrapped = pltpu.roll(merged, 1, axis=1)
    acc = b.reshape(1, 1, n) + x3 * w[CONV_WIDTH - 1:CONV_WIDTH, :].reshape(1, 1, n)
    for d in range(1, CONV_WIDTH):
        shifted = jnp.concatenate([wrapped[tail - d:], x3[:REGS - d]], axis=0)
        acc = acc + shifted * w[CONV_WIDTH - 1 - d:CONV_WIDTH - d, :].reshape(1, 1, n)
    return _silu_of_twice(acc).reshape(CHUNK, n)


def _chunk_has_predecessor(k, seq_start):
    return jnp.logical_not(jnp.logical_and(seq_start, k == 0))


def _ssd_group_front(k, g, seq_start, ssd_refs):
    (xs_ref, xs_h_ref, b_ref, b_h_ref, c_ref, c_h_ref, _, _, cw_ref, cb_ref) = ssd_refs[:10]
    rows = slice(k * CHUNK, (k + 1) * CHUNK)
    use_halo = _chunk_has_predecessor(k, seq_start)

    def conv_of(cur_ref, halo_ref, cols, wcols):
        if k == 0:
            halo = halo_ref[0, :, cols]
        else:
            halo = cur_ref[0, k * CHUNK - CONV_HALO:k * CHUNK, cols]
        return _conv_silu(halo, cur_ref[0, rows, cols], cw_ref[:, wcols], cb_ref[:, wcols], use_halo)

    xcols = slice(g * GROUP_X, (g + 1) * GROUP_X)
    ncols = slice(g * D_STATE, (g + 1) * D_STATE)
    bc_b = slice(D_INNER + g * D_STATE, D_INNER + (g + 1) * D_STATE)
    bc_c = slice(D_INNER + SSM_GROUPS * D_STATE + g * D_STATE,
                 D_INNER + SSM_GROUPS * D_STATE + (g + 1) * D_STATE)
    xg = conv_of(xs_ref, xs_h_ref, xcols, xcols)
    bg = conv_of(b_ref, b_h_ref, ncols, bc_b).astype(bf16)
    cg = conv_of(c_ref, c_h_ref, ncols, bc_c)
    cb = jax.lax.dot_general(cg.astype(bf16), bg, NT_DIMS, preferred_element_type=f32)
    return xg, bg, cg, cb


def _ssd_prologue(k, seq_start, ssd_refs):
    dt_ref, dtb_ref, alog_ref, eexp_ref = ssd_refs[7], ssd_refs[10], ssd_refs[11], ssd_refs[14]
    rows = slice(k * CHUNK, (k + 1) * CHUNK)
    use_halo = _chunk_has_predecessor(k, seq_start)

    tok = _token_of(jax.lax.broadcasted_iota(jnp.int32, (1, CHUNK), 1))
    valid_t = jnp.logical_or(use_halo, tok >= PAD_FRONT)
    raw_t = (dt_ref[0, rows, :] + dtb_ref[...]).T[0:SSM_HEADS, :]
    dt_t = jnp.where(valid_t, _softplus(raw_t), 0.0)
    a_col = -jnp.exp(alog_ref[...])
    tok_r = _token_of(jax.lax.broadcasted_iota(jnp.int32, (CHUNK, CHUNK), 0))
    tok_c = _token_of(jax.lax.broadcasted_iota(jnp.int32, (CHUNK, CHUNK), 1))
    causal = tok_r >= tok_c
    tri_t = jnp.where(tok_r <= tok_c, 1.0, 0.0).astype(bf16)
    acs_t = _dot_exact_rhs(dt_t * a_col, tri_t)
    yield
    acs_last = acs_t[:, CHUNK - 1:CHUNK]
    rowarg_t = acs_t * LOG2E - jnp.log(dt_t) * LOG2E
    w_state_t = jnp.exp(acs_last - acs_t) * dt_t
    e_acs_last = jnp.exp(acs_last)

    def tokens_on_rows(x):
        pad = jnp.zeros((CHUNK - SSM_HEADS, CHUNK), f32)
        return jnp.concatenate([x, pad], axis=0).T

    acs2 = tokens_on_rows(acs_t * LOG2E)
    w_exp = _dot(tokens_on_rows(w_state_t).astype(bf16), eexp_ref[...])
    yield
    chunk_decay =tokens_on_rows(jnp.broadcast_to(e_acs_last, (SSM_HEADS, CHUNK)))[0:SUBLANES, :]
    cd_exp = _dot_exact_rhs(chunk_decay, eexp_ref[...])[0:1, :]

    return acs2, rowarg_t, w_exp, cd_exp, causal


def _ssd_group_lhs(g, front, prologue):
    _, _, cg, cb = front
    acs2, rowarg_t, _, _, causal = prologue
    lhs = []
    cb_b, cg_b = cb.astype(bf16), cg.astype(bf16)
    for r in range(HEADS_PER_GROUP):
        h = g * HEADS_PER_GROUP + r
        col = jnp.broadcast_to(acs2[:, h:h + 1], (CHUNK, CHUNK))
        lm = jnp.exp2(col - rowarg_t[h:h + 1, :])
        m = jnp.where(causal, lm, 0.0).astype(bf16) * cb_b
        ce = cg_b * jnp.exp2(col).astype(bf16)
        lhs.append(jnp.concatenate([m, ce], axis=1))
    return lhs


def _ssd_group_back(k, g, front, lhs, prologue, ssd_refs, state_scr, y_ref):
    z_ref, dskip_ref, nw_ref = ssd_refs[6], ssd_refs[12], ssd_refs[13]
    xg, bg, _, _ = front
    _, _, w_exp, cd_exp, _ = prologue
    rows = slice(k * CHUNK, (k + 1) * CHUNK)
    xcols = slice(g * GROUP_X, (g + 1) * GROUP_X)
    low_head = jax.lax.broadcasted_iota(jnp.int32, (1, LANES), 1) < SSM_HEAD_DIM
    xg_b = xg.astype(bf16)
    s_old = state_scr[g]
    s_b = s_old.astype(bf16)

    halves = []
    for half in range(HEADS_PER_GROUP // 2):
        hcols = slice(half * LANES, (half + 1) * LANES)
        rhs = jnp.concatenate([xg_b[:, hcols], s_b[:, hcols]], axis=0)
        outs = [_dot(lhs[r], rhs) for r in (2 * half, 2 * half + 1)]
        halves.append(jnp.where(low_head, outs[0], outs[1]))
    y = jnp.concatenate(halves, axis=1)

    xw = (xg * w_exp[:, xcols]).astype(bf16)
    st = jax.lax.dot_general(bg, xw, (((0,), (0,)), ((), ())), preferred_element_type=f32)
    state_scr[g] = s_old * cd_exp[:, xcols] + st

    y = y + xg * dskip_ref[:, xcols]
    y = y * _silu_of_twice(z_ref[0, rows, xcols].astype(f32))
    ms = jnp.mean(y * y, axis=-1, keepdims=True)
    y = y * jax.lax.rsqrt(ms + EPS) * nw_ref[:, xcols]
    y_ref[rows, xcols] = y.astype(bf16)


def _ssd_tile(seq_start, ssd_refs, state_scr, y_ref):
    n_chunks = TM_MLP // CHUNK
    items = [(k, g) for k in range(n_chunks) for g in range(SSM_GROUPS)]

    def advance(k):
        try:
            next(pending_prologue[k])
        except StopIteration as done:
            prologues[k] = done.value

    prologues, fronts = {}, {}
    pending_prologue = {k: _ssd_prologue(k, seq_start, ssd_refs) for k in range(n_chunks)}
    for _ in range(PROLOGUE_PARTS - 1):
        advance(0)
        yield
    advance(0)
    fronts[items[0]] = _ssd_group_front(*items[0], seq_start, ssd_refs)
    yield
    for n, (k, g) in enumerate(items):
        if n + 1 < len(items):
            fronts[items[n + 1]] = _ssd_group_front(*items[n + 1], seq_start, ssd_refs)
        if g in NEXT_PROLOGUE_AT and k + 1 < n_chunks:
            advance(k + 1)
        front = fronts.pop((k, g))
        lhs = _ssd_group_lhs(g, front, prologues[k])
        _ssd_group_back(k, g, front, lhs, prologues[k], ssd_refs, state_scr, y_ref)
        yield


def _merge_mlp_body(h_ref, gate_ref, ypool_ref, ynorm_ref, bgate_ref, wssd_ref, wo_ref,
                    mw_ref, ff1_ref, ff2_ref, fw_ref, o_ref, final):
    def dot_in_pieces(a, w_ref, row0, col0, n_cols, col_pieces):
        kw = a.shape[1] // MERGE_K_PIECES
        cw = n_cols // col_pieces
        out = []
        for p in range(col_pieces):
            acc = None
            for q in range(MERGE_K_PIECES):
                part = _dot(a[:, q * kw:(q + 1) * kw],
                            w_ref[row0 + q * kw:row0 + (q + 1) * kw, col0 + p * cw:col0 + (p + 1) * cw])
                acc = part if acc is None else acc + part
                yield
            out.append(acc)
        return jnp.concatenate(out, axis=1)

    y_ssd = yield from dot_in_pieces(ynorm_ref[...], wssd_ref, 0, 0, D_MODEL, 4)
    gates = _sigmoid(gate_ref[0].astype(f32) + bgate_ref[...])
    mix = (gates[:, :D_MODEL] * ypool_ref[0].astype(f32) + gates[:, D_MODEL:] * y_ssd).astype(bf16)
    h1 = h_ref[0] + (yield from dot_in_pieces(mix, wo_ref, 0, 0, D_MODEL, 2))
    ms = jnp.mean(h1 * h1, axis=-1, keepdims=True)
    v = (h1 * jax.lax.rsqrt(ms + EPS) * mw_ref[...]).astype(bf16)
    acc = h1
    for k in range(D_FF // FF_CHUNK):
        hid = jnp.maximum((yield from dot_in_pieces(v, ff1_ref, 0, k * FF_CHUNK, FF_CHUNK, 2)), 0.0)
        hid = (hid * hid).astype(bf16)
        acc = acc + (yield from dot_in_pieces(hid, ff2_ref, k * FF_CHUNK, 0, D_MODEL, 2))
    if final:
        ms = jnp.mean(acc * acc, axis=-1, keepdims=True)
        acc = acc * jax.lax.rsqrt(ms + EPS) * fw_ref[...]
    o_ref[0] = acc


N_SSD_REFS = 15
SCAN_STAGES_PER_MERGE_STAGE = 1
MERGE_K_PIECES = 1
PROLOGUE_PARTS = 3
NEXT_PROLOGUE_AT = (0, 2, 4)
_DONE = object()


def _mixer_mlp_kernel(*refs, final, tiles_per_row):
    ssd_refs = refs[:N_SSD_REFS]
    (h_ref, gate_ref, ypool_ref, bgate_ref, wssd_ref, wo_ref, mw_ref, ff1_ref, ff2_ref, fw_ref,
     o_ref, state_scr, ybuf) = refs[N_SSD_REFS:]
    t = pl.program_id(0)
    last = pl.num_programs(0) - 1
    seq_start = jnp.logical_and(t % tiles_per_row == 0, t < last)

    @pl.when(seq_start)
    def _():
        state_scr[...] = jnp.zeros_like(state_scr)

    @pl.when(t == 0)
    def _():
        ybuf[...] = jnp.zeros_like(ybuf)

    write_slot = jnp.bitwise_and(t, 1)
    merge = _merge_mlp_body(h_ref, gate_ref, ypool_ref, ybuf.at[1 - write_slot], bgate_ref, wssd_ref,
                            wo_ref, mw_ref, ff1_ref, ff2_ref, fw_ref, o_ref, final)
    scan = _ssd_tile(seq_start, ssd_refs, state_scr, ybuf.at[write_slot])
    pending = [scan, merge]
    while pending:
        for gen, stages in ((scan, SCAN_STAGES_PER_MERGE_STAGE), (merge, 1)):
            for _ in range(stages):
                if gen in pending and next(gen, _DONE) is _DONE:
                    pending.remove(gen)


def _mixer_mlp(layer, h, proj, dt_raw, ypool, cw, cb, dtb, alog, dskip, ssd_nw, eexp,
               bgate, wssd, wo, mw, ff1, ff2, fw):
    bsz, lp, _ = h.shape
    n_tiles = lp // TM_MLP
    n_total = bsz * n_tiles
    hb = TM_MLP // CONV_HALO

    def scan_tile(t):
        tile = jnp.minimum(t, n_total - 1)
        return tile // n_tiles, tile % n_tiles

    def merge_tile(t):
        tile = jnp.maximum(t - 1, 0)
        return tile // n_tiles, tile % n_tiles

    def cur(width, col):
        return pl.BlockSpec((1, TM_MLP, width), lambda t: (*scan_tile(t), col // width))

    def halo(width, col):
        def index(t):
            b, i = scan_tile(t)
            return b, jnp.maximum(i * hb - 1, 0), col // width
        return pl.BlockSpec((1, CONV_HALO, width), index)

    def const(shape):
        return pl.BlockSpec((None,) + shape, lambda t: (layer,) + (0,) * len(shape),
                            pipeline_mode=pl.Buffered(1))

    def resident(shape):
        return pl.BlockSpec(shape, lambda t: (0,) * len(shape), pipeline_mode=pl.Buffered(1))

    def merge_in(width, col):
        return pl.BlockSpec((1, TM_MLP, width), lambda t: (*merge_tile(t), col // width))

    n_bc = SSM_GROUPS * D_STATE
    return pl.pallas_call(
        functools.partial(_mixer_mlp_kernel, final=layer == DEPTH - 1, tiles_per_row=n_tiles),
        grid=(n_total + 1,),
        in_specs=[
            cur(D_INNER, COL_XS), halo(D_INNER, COL_XS),
            cur(n_bc, COL_B), halo(n_bc, COL_B),
            cur(n_bc, COL_C), halo(n_bc, COL_C),
            cur(D_INNER, COL_Z),
            pl.BlockSpec((1, TM_MLP, DT_LANES), lambda t: (*scan_tile(t), 0)),
            const((CONV_WIDTH, D_INNER + 2 * n_bc)),
            const((1, D_INNER + 2 * n_bc)),
            const((1, DT_LANES)),
            const((SSM_HEADS, 1)),
            const((1, D_INNER)),
            const((1, D_INNER)),
            resident((DT_LANES, D_INNER)),
            merge_in(D_MODEL, 0),
            merge_in(2 * D_MODEL, COL_GATE),
            merge_in(D_MODEL, 0),
            const((1, 2 * D_MODEL)),
            resident((D_INNER, D_MODEL)),
            resident((D_MODEL, D_MODEL)),
            const((1, D_MODEL)),
            resident((D_MODEL, D_FF)),
            resident((D_FF, D_MODEL)),
            resident((1, D_MODEL)),
        ],
        out_specs=pl.BlockSpec((1, TM_MLP, D_MODEL), lambda t: (*merge_tile(t), 0)),
        out_shape=jax.ShapeDtypeStruct((bsz, lp, D_MODEL), f32),
        scratch_shapes=[
            pltpu.VMEM((SSM_GROUPS, D_STATE, GROUP_X), f32),
            pltpu.VMEM((2, TM_MLP, D_INNER), bf16),
        ],
        compiler_params=pltpu.CompilerParams(
            dimension_semantics=("arbitrary",),
            vmem_limit_bytes=MIXER_VMEM_LIMIT),
        name="mixer_mlp",
    )(proj, proj, proj, proj, proj, proj, proj, dt_raw, cw, cb, dtb, alog, dskip, ssd_nw, eexp,
      h, proj, ypool, bgate, wssd, wo, mw, ff1, ff2, fw)


def _to_strided_rows(t):
    bsz, n, d = t.shape
    t = t.reshape(bsz, n // CHUNK, SUBLANES, REGS, d)
    return jnp.swapaxes(t, 2, 3).reshape(bsz, n, d)


def _from_strided_rows(t):
    bsz, n, d = t.shape
    t = t.reshape(bsz, n // CHUNK, REGS, SUBLANES, d)
    return jnp.swapaxes(t, 2, 3).reshape(bsz, n, d)


def kernel(x, meta_tokens, mix_norm_w, w_in, b_gate, pool_w_group, pool_scale, w_pool_up,
           conv_w, conv_b, dt_bias, a_log, d_skip, ssd_norm_w, w_ssd_out, w_o,
           mlp_norm_w, w_ff1, w_ff2, final_norm_w):
    bsz, seq, _ = x.shape
    meta = jnp.broadcast_to(meta_tokens[None].astype(x.dtype), (bsz, N_META, D_MODEL))
    first_chunk = jnp.concatenate([jnp.zeros((bsz, PAD_FRONT, D_MODEL), x.dtype), meta], axis=1)
    h = _to_strided_rows(jnp.concatenate([first_chunk, x], axis=1))

    w_t = jnp.swapaxes(w_in, 1, 2)
    row = lambda p: p[:, None, :]
    dtb = row(jnp.pad(dt_bias, ((0, 0), (0, DT_LANES - SSM_HEADS))))
    dskip = row(jnp.repeat(d_skip, SSM_HEAD_DIM, axis=-1))
    eexp = (jnp.arange(DT_LANES)[:, None] == (jnp.arange(D_INNER)[None, :] // SSM_HEAD_DIM)).astype(bf16)
    cw = conv_w * 0.5
    cb = row(conv_b * 0.5)
    alog = a_log[:, :, None]
    mix_nw, pscale, ssd_nw = row(mix_norm_w), row(pool_scale), row(ssd_norm_w)
    bgate, mlp_nw = row(b_gate), row(mlp_norm_w)
    to_cast = (w_ff1, w_ff2, w_ssd_out, w_o, w_pool_up, pool_w_group)

    for i in range(DEPTH):
        proj, dt_raw, (ff1, ff2, wssd, wo, wup, wg) = _in_proj(i, h, mix_nw, w_t, to_cast)
        ypool = _pool(i, proj, wg, pscale, wup)
        h = _mixer_mlp(i, h, proj, dt_raw, ypool, cw, cb, dtb, alog, dskip, ssd_nw, eexp,
                       bgate, wssd, wo, mlp_nw, ff1, ff2, final_norm_w[None])
    return _from_strided_rows(h)[:, h.shape[1] - seq:]
```
